```python
import math
import jax
import jax.numpy as jnp
from jax import lax
import numpy as np

D_MODEL = 1024
BATCH = 8
SEQ = 4096
DEPTH = 2

HEAD_DIM = 64
ROT_DIM = HEAD_DIM // 4
ROPE_THETA = 500000.0
NORM_EPS = 1e-6
Q_BLOCK = 128
NEG_INF = -1e30
FORCED = 1e9

NSA_HEADS = 4
NSA_CMP_LEN = 32
NSA_CMP_STRIDE = 16
NSA_CMP_HIDDEN = 256
NSA_SLC_LEN = 64
NSA_TOPK = 16
NSA_WINDOW = 512
NSA_W = NSA_HEADS * HEAD_DIM

S5_GROUPS = 16
S5_GROUP_CH = 16
S5_W = S5_GROUPS * S5_GROUP_CH
S5_STATE = 64

GDN_HEADS = 4
GDN_CONV = 4
GDN_CHUNK = 64
GDN_W = GDN_HEADS * HEAD_DIM

SB_HEADS = 4
SB_W = SB_HEADS * HEAD_DIM

N_BRANCH = 4
BRANCH_W = NSA_W
D_FF = 256 * math.ceil(8 * D_MODEL / (3 * 256))

IN_SPLITS = (NSA_W, 6 * HEAD_DIM, 3 * NSA_HEADS, S5_W, 3 * GDN_W, GDN_HEADS, GDN_HEADS, GDN_W, 3 * SB_W, N_BRANCH * D_MODEL)
D_IN = sum(IN_SPLITS)
IN_OFFSETS = tuple(int(v) for v in np.cumsum(IN_SPLITS)[:-1])

kernel_name = "hybrid_nsa_s5_gdn_stickbreaking_block"


def rms_norm(x, w):
    xf = x.astype(jnp.float32)
    y = xf * lax.rsqrt(jnp.mean(xf * xf, axis=-1, keepdims=True) + NORM_EPS)
    return (y * w.astype(jnp.float32)).astype(x.dtype)


def l2norm(x):
    return x * lax.rsqrt(jnp.sum(x * x, axis=-1, keepdims=True) + NORM_EPS)


def rope_tables(positions):
    half = ROT_DIM // 2
    inv_freq = ROPE_THETA ** (-jnp.arange(half, dtype=jnp.float32) / half)
    ang = positions.astype(jnp.float32)[..., None] * inv_freq
    return jnp.cos(ang), jnp.sin(ang)


def apply_rope(x, cos, sin):
    half = ROT_DIM // 2
    x1, x2, rest = x[..., :half], x[..., half:ROT_DIM], x[..., ROT_DIM:]
    return jnp.concatenate([x1 * cos - x2 * sin, x2 * cos + x1 * sin, rest], axis=-1).astype(x.dtype)


def masked_softmax(s, mask):
    s = jnp.where(mask, s, NEG_INF)
    m = jnp.max(s, axis=-1, keepdims=True)
    p = jnp.where(mask, jnp.exp(s - m), 0.0)
    return p / jnp.maximum(jnp.sum(p, axis=-1, keepdims=True), 1e-30)


def causal_depthwise_conv(x, w):
    k = w.shape[0]
    return lax.conv_general_dilated(x, w[:, None, :], window_strides=(1,), padding=[(k - 1, 0)],
                                    dimension_numbers=('NWC', 'WIO', 'NWC'), feature_group_count=x.shape[-1])


def nsa_mixer(q, kv, gate_logits, cos, sin, q_norm, k_norm, cmp_pos, ck_w1, ck_w2, cv_w1, cv_w2):
    B_, S_ = q.shape[:2]
    f32 = jnp.float32
    scale = HEAD_DIM ** -0.5
    q = rms_norm(q.reshape(B_, S_, NSA_HEADS, HEAD_DIM), q_norm)
    q_rot = apply_rope(q, cos[:, :, None], sin[:, :, None])
    k_c, v_c, k_s, v_s, k_w, v_w = jnp.split(kv, 6, axis=-1)
    n_cmp = (S_ - NSA_CMP_LEN) // NSA_CMP_STRIDE + 1
    cmp_start = jnp.arange(n_cmp) * NSA_CMP_STRIDE
    cmp_idx = cmp_start[:, None] + jnp.arange(NSA_CMP_LEN)[None, :]

    def compress(t, pos, w1, w2):
        blocks = (t[:, cmp_idx] + pos).reshape(B_, n_cmp, NSA_CMP_LEN * HEAD_DIM)
        return jax.nn.gelu(blocks @ w1) @ w2

    kc = rms_norm(compress(k_c, cmp_pos[0], ck_w1, ck_w2), k_norm[0])
    vc = compress(v_c, cmp_pos[1], cv_w1, cv_w2).astype(f32)
    cmp_end = cmp_start + NSA_CMP_LEN - 1
    n_slc = S_ // NSA_SLC_LEN
    top_n = min(NSA_TOPK, n_slc)
    slc_start = jnp.arange(n_slc) * NSA_SLC_LEN
    overlap = ((cmp_start[:, None] < slc_start[None, :] + NSA_SLC_LEN)
               & (cmp_start[:, None] + NSA_CMP_LEN > slc_start[None, :])).astype(f32)
    ks_blk = apply_rope(rms_norm(k_s, k_norm[1]), cos, sin).reshape(B_, n_slc, NSA_SLC_LEN, HEAD_DIM)
    vs_blk = v_s.reshape(B_, n_slc, NSA_SLC_LEN, HEAD_DIM)
    pad = ((0, 0), (NSA_WINDOW, 0), (0, 0))
    kw = jnp.pad(apply_rope(rms_norm(k_w, k_norm[2]), cos, sin), pad)
    vw = jnp.pad(v_w, pad)
    gates = jax.nn.sigmoid(gate_logits.astype(f32)).reshape(B_, S_, NSA_HEADS, 3)
    b_idx = jnp.arange(B_)[:, None, None]
    within = jnp.arange(NSA_SLC_LEN)

    def block(c):
        t0 = c * Q_BLOCK
        t = t0 + jnp.arange(Q_BLOCK)
        qb = lax.dynamic_slice_in_dim(q, t0, Q_BLOCK, axis=1)
        qrb = lax.dynamic_slice_in_dim(q_rot, t0, Q_BLOCK, axis=1)
        gb = lax.dynamic_slice_in_dim(gates, t0, Q_BLOCK, axis=1)
        s_c = jnp.einsum('bqhd,bnd->bhqn', qb, kc, preferred_element_type=f32) * scale
        p_c = masked_softmax(s_c, cmp_end[None, :] <= t[:, None])
        o_c = jnp.einsum('bhqn,bnd->bqhd', p_c, vc)
        imp = jnp.einsum('bhqn,nj->bqj', p_c, overlap)
        forced = (slc_start[None, :] == ((t // NSA_SLC_LEN) * NSA_SLC_LEN)[:, None]) | (slc_start[None, :] == 0)
        imp = jnp.where(forced, FORCED, jnp.where(slc_start[None, :] <= t[:, None], imp, NEG_INF))
        _, sel = lax.top_k(imp, top_n)
        kg = ks_blk[b_idx, sel].reshape(B_, Q_BLOCK, top_n * NSA_SLC_LEN, HEAD_DIM)
        vg = vs_blk[b_idx, sel].reshape(B_, Q_BLOCK, top_n * NSA_SLC_LEN, HEAD_DIM).astype(f32)
        kpos = (sel[..., None] * NSA_SLC_LEN + within).reshape(B_, Q_BLOCK, top_n * NSA_SLC_LEN)
        s_s = jnp.einsum('bqhd,bqkd->bhqk', qrb, kg, preferred_element_type=f32) * scale
        p_s = masked_softmax(s_s, (kpos <= t[None, :, None])[:, None])
        o_s = jnp.einsum('bhqk,bqkd->bqhd', p_s, vg)
        kwb = lax.dynamic_slice_in_dim(kw, t0, Q_BLOCK + NSA_WINDOW, axis=1)
        vwb = lax.dynamic_slice_in_dim(vw, t0, Q_BLOCK + NSA_WINDOW, axis=1).astype(f32)
        wpos = t0 - NSA_WINDOW + jnp.arange(Q_BLOCK + NSA_WINDOW)
        diff = t[:, None] - wpos[None, :]
        wmask = (diff >= 0) & (diff < NSA_WINDOW) & (wpos[None, :] >= 0)
        s_w = jnp.einsum('bqhd,bkd->bhqk', qrb, kwb, preferred_element_type=f32) * scale
        p_w = masked_softmax(s_w, wmask)
        o_w = jnp.einsum('bhqk,bkd->bqhd', p_w, vwb)
        return gb[..., 0:1] * o_c + gb[..., 1:2] * o_s + gb[..., 2:3] * o_w

    out = lax.map(block, jnp.arange(S_ // Q_BLOCK))
    return jnp.moveaxis(out, 0, 1).reshape(B_, S_, NSA_W).astype(q.dtype)


def s5_mixer(u, lam_re, lam_im, log_dt, b_re, b_im, c_re, c_im, d_skip, glu_w):
    B_, S_ = u.shape[:2]
    f32 = jnp.float32
    uf = u.astype(f32).reshape(B_, S_, S5_GROUPS, S5_GROUP_CH)
    dt = jnp.exp(log_dt.astype(f32))[:, None]
    lr, li = lam_re.astype(f32), lam_im.astype(f32)
    mag = jnp.exp(lr * dt)
    ab_re, ab_im = mag * jnp.cos(li * dt), mag * jnp.sin(li * dt)
    den = lr * lr + li * li
    f_re = ((ab_re - 1.0) * lr + ab_im * li) / den
    f_im = (ab_im * lr - (ab_re - 1.0) * li) / den
    br, bi = b_re.astype(f32), b_im.astype(f32)
    bb_re = f_re[..., None] * br - f_im[..., None] * bi
    bb_im = f_re[..., None] * bi + f_im[..., None] * br
    bu_re = jnp.einsum('bsgc,gpc->bsgp', uf, bb_re)
    bu_im = jnp.einsum('bsgc,gpc->bsgp', uf, bb_im)
    a_re = jnp.broadcast_to(ab_re, bu_re.shape)
    a_im = jnp.broadcast_to(ab_im, bu_im.shape)

    def combine(e1, e2):
        a1r, a1i, b1r, b1i = e1
        a2r, a2i, b2r, b2i = e2
        return (a2r * a1r - a2i * a1i, a2r * a1i + a2i * a1r,
                a2r * b1r - a2i * b1i + b2r, a2r * b1i + a2i * b1r + b2i)

    _, _, x_re, x_im = lax.associative_scan(combine, (a_re, a_im, bu_re, bu_im), axis=1)
    y = (jnp.einsum('gcp,bsgp->bsgc', c_re.astype(f32), x_re)
         - jnp.einsum('gcp,bsgp->bsgc', c_im.astype(f32), x_im)
         + d_skip.astype(f32).reshape(S5_GROUPS, S5_GROUP_CH) * uf)
    y = jax.nn.gelu(y.reshape(B_, S_, S5_W))
    z = y @ glu_w.astype(f32)
    return (z[..., :S5_W] * jax.nn.sigmoid(z[..., S5_W:])).astype(u.dtype)


def chunk_gated_delta_rule(q, k, v, g, beta):
    B_, S_, H, d = q.shape
    C = GDN_CHUNK
    N = S_ // C

    def to_chunks(t):
        return jnp.moveaxis(t.reshape((B_, N, C, H) + t.shape[3:]), 3, 2)

    q, k, v, g, beta = [to_chunks(t) for t in (q, k, v, g, beta)]
    G = jnp.cumsum(g, axis=-1)
    tril = jnp.tril(jnp.ones((C, C), dtype=bool))
    strict = jnp.tril(jnp.ones((C, C), dtype=bool), -1)
    decay = jnp.where(tril, jnp.exp(jnp.where(tril, G[..., :, None] - G[..., None, :], 0.0)), 0.0)
    kb = k * beta[..., None]
    vb = v * beta[..., None]
    L = jnp.where(strict, jnp.einsum('bnhid,bnhjd->bnhij', kb, k) * decay, 0.0)
    eye = jnp.eye(C, dtype=L.dtype)
    T = lax.linalg.triangular_solve(eye + L, jnp.broadcast_to(eye, L.shape), left_side=True,
                                    lower=True, unit_diagonal=True)
    u = jnp.einsum('bnhij,bnhjd->bnhid', T, vb)
    w = jnp.einsum('bnhij,bnhjd->bnhid', T, kb * jnp.exp(G)[..., None])
    a_intra = jnp.where(tril, jnp.einsum('bnhid,bnhjd->bnhij', q, k) * decay, 0.0)
    q_dec = q * jnp.exp(G)[..., None]
    k_dec = k * jnp.exp(G[..., -1:] - G)[..., None]
    g_last = jnp.exp(G[..., -1])

    def step(state, xs):
        u_i, w_i, a_i, qd_i, kd_i, gl_i = xs
        v_new = u_i - jnp.einsum('bhcd,bhde->bhce', w_i, state)
        o = jnp.einsum('bhcd,bhde->bhce', qd_i, state) + jnp.einsum('bhij,bhje->bhie', a_i, v_new)
        state = state * gl_i[..., None, None] + jnp.einsum('bhcd,bhce->bhde', kd_i, v_new)
        return state, o

    xs = tuple(jnp.moveaxis(t, 1, 0) for t in (u, w, a_intra, q_dec, k_dec, g_last))
    state0 = jnp.zeros((B_, H, d, v.shape[-1]), dtype=q.dtype)
    _, o = lax.scan(step, state0, xs)
    return o.transpose(1, 0, 3, 2, 4).reshape(B_, S_, H, v.shape[-1])


def gdn_mixer(qkv, a_in, b_in, z, conv_w, a_log, dt_bias, o_norm):
    B_, S_ = qkv.shape[:2]
    f32 = jnp.float32
    act = jax.nn.silu(causal_depthwise_conv(qkv, conv_w))
    q, k, v = [t.reshape(B_, S_, GDN_HEADS, HEAD_DIM).astype(f32) for t in jnp.split(act, 3, axis=-1)]
    q = l2norm(q) * HEAD_DIM ** -0.5
    k = l2norm(k)
    beta = jax.nn.sigmoid(b_in.astype(f32))
    g = -jnp.exp(a_log.astype(f32)) * jax.nn.softplus(a_in.astype(f32) + dt_bias.astype(f32))
    o = chunk_gated_delta_rule(q, k, v, g, beta)
    o = rms_norm(o, o_norm) * jax.nn.silu(z.astype(f32).reshape(B_, S_, GDN_HEADS, HEAD_DIM))
    return o.reshape(B_, S_, GDN_W).astype(qkv.dtype)


def stickbreaking_mixer(qkv):
    B_, S_ = qkv.shape[:2]
    f32 = jnp.float32
    q, k, v = [t.reshape(B_, S_, SB_HEADS, HEAD_DIM) for t in jnp.split(qkv, 3, axis=-1)]
    scale = HEAD_DIM ** -0.5
    outs = []
    for c in range(S_ // Q_BLOCK):
        t0 = c * Q_BLOCK
        n_k = t0 + Q_BLOCK
        z = jnp.einsum('bqhd,bkhd->bhqk', q[:, t0:n_k], k[:, :n_k], preferred_element_type=f32) * scale
        t = t0 + jnp.arange(Q_BLOCK)
        mask = jnp.arange(n_k)[None, :] < t[:, None]
        log_nb = jnp.where(mask, -jax.nn.softplus(z), 0.0)
        after = lax.cumsum(log_nb, axis=3, reverse=True) - log_nb
        weights = jnp.where(mask, jnp.exp(jax.nn.log_sigmoid(z) + after), 0.0)
        outs.append(jnp.einsum('bhqk,bkhd->bqhd', weights, v[:, :n_k].astype(f32)))
    return jnp.concatenate(outs, axis=1).reshape(B_, S_, SB_W).astype(qkv.dtype)


def swiglu_ffn(x, w_gate, w_up, w_down):
    return (jax.nn.silu(x @ w_gate) * (x @ w_up)) @ w_down


def setup_inputs(seed: int = 0) -> dict:
    key = jax.random.key(seed)
    keys = iter(jax.random.split(key, 40))
    f32 = jnp.float32
    L = DEPTH

    def nrm(shape, scale):
        return jax.random.normal(next(keys), shape, f32) * scale

    def gain(shape):
        return 1.0 + nrm(shape, 0.05)

    def unif(shape, lo, hi):
        return jax.random.uniform(next(keys), shape, f32, minval=lo, maxval=hi)

    x = nrm((BATCH, SEQ, D_MODEL), 1.0)
    positions = (jnp.arange(SEQ, dtype=jnp.int32)[None, :]
                 + jax.random.randint(next(keys), (BATCH, 1), 0, 1024, dtype=jnp.int32))
    dt_g = jnp.exp(unif((L, GDN_HEADS), math.log(1e-3), math.log(1e-1)))
    return {
        'x': x,
        'positions': positions,
        'attn_norm': gain((L, D_MODEL)),
        'w_in': nrm((L, D_MODEL, D_IN), D_MODEL ** -0.5),
        'nsa_q_norm': gain((L, HEAD_DIM)),
        'nsa_k_norm': gain((L, 3, HEAD_DIM)),
        'nsa_cmp_pos': nrm((L, 2, NSA_CMP_LEN, HEAD_DIM), 0.1),
        'nsa_cmp_k_w1': nrm((L, NSA_CMP_LEN * HEAD_DIM, NSA_CMP_HIDDEN), (NSA_CMP_LEN * HEAD_DIM) ** -0.5),
        'nsa_cmp_k_w2': nrm((L, NSA_CMP_HIDDEN, HEAD_DIM), NSA_CMP_HIDDEN ** -0.5),
        'nsa_cmp_v_w1': nrm((L, NSA_CMP_LEN * HEAD_DIM, NSA_CMP_HIDDEN), (NSA_CMP_LEN * HEAD_DIM) ** -0.5),
        'nsa_cmp_v_w2': nrm((L, NSA_CMP_HIDDEN, HEAD_DIM), NSA_CMP_HIDDEN ** -0.5),
        's5_lam_re': -0.5 + nrm((L, S5_GROUPS, S5_STATE), 0.01),
        's5_lam_im': math.pi * jnp.arange(S5_STATE, dtype=f32) + nrm((L, S5_GROUPS, S5_STATE), 0.01),
        's5_log_dt': unif((L, S5_GROUPS), math.log(1e-3), math.log(1e-1)),
        's5_b_re': nrm((L, S5_GROUPS, S5_STATE, S5_GROUP_CH), (2 * S5_GROUP_CH) ** -0.5),
        's5_b_im': nrm((L, S5_GROUPS, S5_STATE, S5_GROUP_CH), (2 * S5_GROUP_CH) ** -0.5),
        's5_c_re': nrm((L, S5_GROUPS, S5_GROUP_CH, S5_STATE), (2 * S5_STATE) ** -0.5),
        's5_c_im': nrm((L, S5_GROUPS, S5_GROUP_CH, S5_STATE), (2 * S5_STATE) ** -0.5),
        's5_d': nrm((L, S5_W), 1.0),
        's5_glu_w': nrm((L, S5_W, 2 * S5_W), S5_W ** -0.5),
        'gdn_conv_w': nrm((L, GDN_CONV, 3 * GDN_W), GDN_CONV ** -0.5),
        'gdn_a_log': jnp.log(unif((L, GDN_HEADS), 1.0, 16.0)),
        'gdn_dt_bias': dt_g + jnp.log(-jnp.expm1(-dt_g)),
        'gdn_o_norm': gain((L, HEAD_DIM)),
        'w_branch': nrm((L, N_BRANCH, BRANCH_W, D_MODEL), BRANCH_W ** -0.5),
        'w_out': nrm((L, D_MODEL, D_MODEL), D_MODEL ** -0.5),
        'ffn_norm': gain((L, D_MODEL)),
        'w_gate': nrm((L, D_MODEL, D_FF), D_MODEL ** -0.5),
        'w_up': nrm((L, D_MODEL, D_FF), D_MODEL ** -0.5),
        'w_down': nrm((L, D_FF, D_MODEL), D_FF ** -0.5),
    }


def reference(x, positions, attn_norm, w_in, nsa_q_norm, nsa_k_norm, nsa_cmp_pos, nsa_cmp_k_w1, nsa_cmp_k_w2,
              nsa_cmp_v_w1, nsa_cmp_v_w2, s5_lam_re, s5_lam_im, s5_log_dt, s5_b_re, s5_b_im, s5_c_re, s5_c_im,
              s5_d, s5_glu_w, gdn_conv_w, gdn_a_log, gdn_dt_bias, gdn_o_norm, w_branch, w_out, ffn_norm,
              w_gate, w_up, w_down):
    B_, S_, _ = x.shape
    cos, sin = rope_tables(positions)
    for l in range(DEPTH):
        h = rms_norm(x, attn_norm[l])
        proj = h @ w_in[l]
        (nsa_q, nsa_kv, nsa_g, s5_u, gdn_qkv, gdn_a, gdn_b, gdn_z, sb_qkv,
         merge_logits) = jnp.split(proj, IN_OFFSETS, axis=-1)
        o_nsa = nsa_mixer(nsa_q, nsa_kv, nsa_g, cos, sin, nsa_q_norm[l], nsa_k_norm[l], nsa_cmp_pos[l],
                          nsa_cmp_k_w1[l], nsa_cmp_k_w2[l], nsa_cmp_v_w1[l], nsa_cmp_v_w2[l])
        o_s5 = s5_mixer(s5_u, s5_lam_re[l], s5_lam_im[l], s5_log_dt[l], s5_b_re[l], s5_b_im[l],
                        s5_c_re[l], s5_c_im[l], s5_d[l], s5_glu_w[l])
        o_gdn = gdn_mixer(gdn_qkv, gdn_a, gdn_b, gdn_z, gdn_conv_w[l], gdn_a_log[l], gdn_dt_bias[l], gdn_o_norm[l])
        o_sb = stickbreaking_mixer(sb_qkv)
        gates = jax.nn.sigmoid(merge_logits.astype(jnp.float32)).reshape(B_, S_, N_BRANCH, D_MODEL).astype(x.dtype)
        merged = jnp.zeros_like(x)
        for m, o_m in enumerate((o_nsa, o_s5, o_gdn, o_sb)):
            merged = merged + gates[:, :, m] * (o_m @ w_branch[l, m])
        x = x + merged @ w_out[l]
        x = x + swiglu_ffn(rms_norm(x, ffn_norm[l]), w_gate[l], w_up[l], w_down[l])
    return x
```

```python
import functools
import math

import numpy as np
import jax
import jax.numpy as jnp
from jax import lax
from jax.experimental import pallas as pl
from jax.experimental.pallas import tpu as pltpu

F32 = jnp.float32
BF16 = jnp.bfloat16

D_MODEL = 1024
HEAD_DIM = 64
ROT_DIM = HEAD_DIM // 4
ROPE_THETA = 500000.0
NORM_EPS = 1e-6
Q_BLOCK = 128
NEG_INF = -1e30
FORCED = 1e9

NSA_HEADS = 4
NSA_CMP_LEN = 32
NSA_CMP_STRIDE = 16
NSA_SLC_LEN = 64
NSA_TOPK = 16
NSA_WINDOW = 512
NSA_W = NSA_HEADS * HEAD_DIM

S5_GROUPS = 16
S5_GROUP_CH = 16
S5_W = S5_GROUPS * S5_GROUP_CH
S5_STATE = 64
S5_LANES = S5_GROUPS * S5_STATE

GDN_HEADS = 4
GDN_CONV = 4
GDN_CHUNK = 64
GDN_W = GDN_HEADS * HEAD_DIM

SB_HEADS = 4
SB_W = SB_HEADS * HEAD_DIM

N_BRANCH = 4
D_FF = 256 * math.ceil(8 * D_MODEL / (3 * 256))

LANES = 128
SUBLANES = 8

SMALL_NSA_G = 0
SMALL_GDN_A = 12
SMALL_GDN_B = 16
SMALL_W = LANES

IN_SPLITS = (NSA_W, 6 * HEAD_DIM, 3 * NSA_HEADS, S5_W, 3 * GDN_W, GDN_HEADS, GDN_HEADS, GDN_W, 3 * SB_W,
             N_BRANCH * D_MODEL)
IN_OFFSETS = tuple(int(v) for v in np.cumsum((0,) + IN_SPLITS))

VMEM_LIMIT = 56 * 1024 * 1024


def _cparams(sem):
    return pltpu.CompilerParams(dimension_semantics=sem, vmem_limit_bytes=VMEM_LIMIT)


def _rms(x, gain):
    return x * lax.rsqrt(jnp.mean(x * x, axis=-1, keepdims=True) + NORM_EPS) * gain


def _gelu_tanh(x):
    return 0.5 * x * (1.0 + jnp.tanh(math.sqrt(2.0 / math.pi) * (x + 0.044715 * (x * x * x))))


def _sigmoid(x):
    return 1.0 / (1.0 + jnp.exp(-x))


def _softplus(x):
    return jnp.maximum(x, 0.0) + jnp.log(1.0 + jnp.exp(-jnp.abs(x)))


def _dot(a, b):
    return jnp.dot(a, b, preferred_element_type=F32)


def _dot_nt(a, b):
    return lax.dot_general(a, b, (((1,), (1,)), ((), ())), preferred_element_type=F32)


def _split2(x):
    hi = x.astype(BF16)
    lo = (x - hi.astype(F32)).astype(BF16)
    return hi, lo


def _split3(x):
    hi = x.astype(BF16)
    r = x - hi.astype(F32)
    mid = r.astype(BF16)
    lo = (r - mid.astype(F32)).astype(BF16)
    return hi, mid, lo


def _dot3(a, b):
    ah, al = _split2(a)
    bh, bl = _split2(b)
    return _dot(ah, bh) + (_dot(ah, bl) + _dot(al, bh))


def _rope64(xn, ct, s1, s2):
    half = ROT_DIM // 2
    left = jnp.concatenate([xn[:, half:], xn[:, :half]], axis=1)
    right = jnp.concatenate([xn[:, HEAD_DIM - half:], xn[:, :HEAD_DIM - half]], axis=1)
    return xn * ct + left * s1 + right * s2


INPROJ_WIDTHS = (NSA_W, 6 * HEAD_DIM, S5_W, 3 * GDN_W, GDN_W, 3 * SB_W, SMALL_W)
INPROJ_DTYPES = (F32, F32, F32, F32, F32, BF16, F32)


def _inproj_kernel(x_ref, g_ref, w_ref, *out_refs):
    h = _rms(x_ref[...], g_ref[...]).astype(BF16)
    off = 0
    for o_ref, width in zip(out_refs, INPROJ_WIDTHS):
        o_ref[...] = _dot(h, w_ref[:, off:off + width]).astype(o_ref.dtype)
        off += width


def _inproj(x2, gain, w_a, tm):
    t, d = x2.shape
    n = w_a.shape[1]
    return pl.pallas_call(
        _inproj_kernel,
        grid=(t // tm,),
        in_specs=[pl.BlockSpec((tm, d), lambda i: (i, 0)),
                  pl.BlockSpec((1, d), lambda i: (0, 0)),
                  pl.BlockSpec((d, n), lambda i: (0, 0))],
        out_specs=[pl.BlockSpec((tm, w), lambda i: (i, 0)) for w in INPROJ_WIDTHS],
        out_shape=[jax.ShapeDtypeStruct((t, w), dt) for w, dt in zip(INPROJ_WIDTHS, INPROJ_DTYPES)],
        compiler_params=_cparams(("parallel",)),
        name="inproj",
    )(x2, gain, w_a)


def _arrange_w_in(w_in_l):
    o = IN_OFFSETS
    small = jnp.concatenate([w_in_l[:, o[2]:o[3]], w_in_l[:, o[5]:o[6]], w_in_l[:, o[6]:o[7]]], axis=1)
    small = jnp.pad(small, ((0, 0), (0, SMALL_W - small.shape[1])))
    cols = [w_in_l[:, o[0]:o[1]], w_in_l[:, o[1]:o[2]], w_in_l[:, o[3]:o[4]], w_in_l[:, o[4]:o[5]],
            w_in_l[:, o[7]:o[8]], w_in_l[:, o[8]:o[9]], small]
    return jnp.concatenate(cols, axis=1).astype(BF16), w_in_l[:, o[9]:o[10]].astype(BF16)


def _merge_kernel(x_ref, g_ref, wg_ref, o1_ref, o2_ref, o3_ref, o4_ref, p_ref, wo_ref, out_ref):
    x = x_ref[...]
    d = x.shape[1]
    h = _rms(x, g_ref[...]).astype(BF16)
    merged = jnp.zeros(x.shape, F32)
    for m, o_ref in enumerate((o1_ref, o2_ref, o3_ref, o4_ref)):
        gate = _sigmoid(_dot(h, wg_ref[:, m * d:(m + 1) * d]))
        merged = merged + gate * _dot(o_ref[...].astype(BF16), p_ref[m])
    out_ref[...] = x + _dot(merged.astype(BF16), wo_ref[...])


def _merge(x2, gain, w_g, outs, w_branch, w_out, tm):
    t, d = x2.shape
    bw = outs[0].shape[1]
    return pl.pallas_call(
        _merge_kernel,
        grid=(t // tm,),
        in_specs=[pl.BlockSpec((tm, d), lambda i: (i, 0)),
                  pl.BlockSpec((1, d), lambda i: (0, 0)),
                  pl.BlockSpec((d, N_BRANCH * d), lambda i: (0, 0))]
                 + [pl.BlockSpec((tm, bw), lambda i: (i, 0)) for _ in range(N_BRANCH)]
                 + [pl.BlockSpec((N_BRANCH, bw, d), lambda i: (0, 0, 0)),
                    pl.BlockSpec((d, d), lambda i: (0, 0))],
        out_specs=pl.BlockSpec((tm, d), lambda i: (i, 0)),
        out_shape=jax.ShapeDtypeStruct((t, d), F32),
        compiler_params=_cparams(("parallel",)),
        name="merge",
    )(x2, gain, w_g, *outs, w_branch, w_out)


def _ffn_kernel(x_ref, g_ref, wg_ref, wu_ref, wd_ref, out_ref, h_scr, acc_scr):
    j = pl.program_id(1)

    @pl.when(j == 0)
    def _():
        h_scr[...] = _rms(x_ref[...], g_ref[...]).astype(BF16)
        acc_scr[...] = jnp.zeros(acc_scr.shape, F32)

    h = h_scr[...]
    a = _dot(h, wg_ref[...])
    u = _dot(h, wu_ref[...])
    act = (a * _sigmoid(a) * u).astype(BF16)
    acc_scr[...] += _dot(act, wd_ref[...])

    @pl.when(j == pl.num_programs(1) - 1)
    def _():
        out_ref[...] = x_ref[...] + acc_scr[...]


def _ffn(x2, gain, w_gate, w_up, w_down, tm, tf):
    t, d = x2.shape
    f = w_gate.shape[1]
    return pl.pallas_call(
        _ffn_kernel,
        grid=(t // tm, f // tf),
        in_specs=[pl.BlockSpec((tm, d), lambda i, j: (i, 0)),
                  pl.BlockSpec((1, d), lambda i, j: (0, 0)),
                  pl.BlockSpec((d, tf), lambda i, j: (0, j)),
                  pl.BlockSpec((d, tf), lambda i, j: (0, j)),
                  pl.BlockSpec((tf, d), lambda i, j: (j, 0))],
        out_specs=pl.BlockSpec((tm, d), lambda i, j: (i, 0)),
        out_shape=jax.ShapeDtypeStruct((t, d), F32),
        scratch_shapes=[pltpu.VMEM((tm, d), BF16), pltpu.VMEM((tm, d), F32)],
        compiler_params=_cparams(("parallel", "arbitrary")),
        name="ffn",
    )(x2, gain, w_gate, w_up, w_down)


def _s5_kernel(u_ref, wb_ref, are_ref, aim_ref, cm_ref, d_ref, glu_ref, out_ref, st_ref, bu_ref):
    nb = st_ref.shape[0]
    p = are_ref.shape[1]
    steps = u_ref.shape[0] // nb

    @pl.when(pl.program_id(0) == 0)
    def _():
        st_ref[...] = jnp.zeros(st_ref.shape, F32)

    u = u_ref[...]
    bu_ref[...] = _dot(u.astype(BF16), wb_ref[...])
    are = jnp.broadcast_to(are_ref[...], (nb, p))
    aim = jnp.broadcast_to(aim_ref[...], (nb, p))

    def body(t, carry):
        xr, xi = carry
        r0 = pl.multiple_of(t * nb, nb)
        br = bu_ref[pl.ds(r0, nb), 0:p]
        bi = bu_ref[pl.ds(r0, nb), p:2 * p]
        nr = are * xr - aim * xi + br
        ni = are * xi + aim * xr + bi
        bu_ref[pl.ds(r0, nb), 0:p] = nr
        bu_ref[pl.ds(r0, nb), p:2 * p] = ni
        return nr, ni

    xr, xi = lax.fori_loop(0, steps, body, (st_ref[:, 0:p], st_ref[:, p:2 * p]))
    st_ref[:, 0:p] = xr
    st_ref[:, p:2 * p] = xi

    y = _dot(bu_ref[...].astype(BF16), cm_ref[...]) + d_ref[...] * u
    y = _gelu_tanh(y)
    z = _dot(y.astype(BF16), glu_ref[...])
    w = z.shape[1] // 2
    out_ref[...] = z[:, :w] * _sigmoid(z[:, w:])


def _s5_params(lam_re, lam_im, log_dt, b_re, b_im, c_re, c_im):
    dt = jnp.exp(log_dt)[:, None]
    mag = jnp.exp(lam_re * dt)
    ab_re, ab_im = mag * jnp.cos(lam_im * dt), mag * jnp.sin(lam_im * dt)
    den = lam_re * lam_re + lam_im * lam_im
    f_re = ((ab_re - 1.0) * lam_re + ab_im * lam_im) / den
    f_im = (ab_im * lam_re - (ab_re - 1.0) * lam_im) / den
    bb_re = f_re[..., None] * b_re - f_im[..., None] * b_im
    bb_im = f_re[..., None] * b_im + f_im[..., None] * b_re
    eye = jnp.eye(S5_GROUPS, dtype=F32)
    wb_re = jnp.einsum('gpc,gh->gchp', bb_re, eye).reshape(S5_W, S5_LANES)
    wb_im = jnp.einsum('gpc,gh->gchp', bb_im, eye).reshape(S5_W, S5_LANES)
    wb = jnp.concatenate([wb_re, wb_im], axis=1).astype(BF16)
    cm_re = jnp.einsum('gcp,gh->hpgc', c_re, eye).reshape(S5_LANES, S5_W)
    cm_im = jnp.einsum('gcp,gh->hpgc', c_im, eye).reshape(S5_LANES, S5_W)
    cm = jnp.concatenate([cm_re, -cm_im], axis=0).astype(BF16)
    return wb, ab_re.reshape(1, S5_LANES), ab_im.reshape(1, S5_LANES), cm


def _s5(u_tm, nb, wb, are, aim, cm, d_skip, glu_w, steps):
    rows, w = u_tm.shape
    r = steps * nb
    p = are.shape[1]
    return pl.pallas_call(
        _s5_kernel,
        grid=(rows // r,),
        in_specs=[pl.BlockSpec((r, w), lambda i: (i, 0)),
                  pl.BlockSpec((w, 2 * p), lambda i: (0, 0)),
                  pl.BlockSpec((1, p), lambda i: (0, 0)),
                  pl.BlockSpec((1, p), lambda i: (0, 0)),
                  pl.BlockSpec((2 * p, w), lambda i: (0, 0)),
                  pl.BlockSpec((1, w), lambda i: (0, 0)),
                  pl.BlockSpec((w, 2 * w), lambda i: (0, 0))],
        out_specs=pl.BlockSpec((r, w), lambda i: (i, 0)),
        out_shape=jax.ShapeDtypeStruct((rows, w), F32),
        scratch_shapes=[pltpu.VMEM((nb, 2 * p), F32), pltpu.VMEM((r, 2 * p), F32)],
        compiler_params=_cparams(("arbitrary",)),
        name="s5",
    )(u_tm, wb, are, aim, cm, d_skip, glu_w)


def _sb_kernel(q_ref, k_ref, v_ref, uu_ref, out_ref, acc_scr, carry_scr):
    c = pl.program_id(2)
    tq = q_ref.shape[2]
    q = q_ref[0, 0]
    acc_scr[...] = jnp.zeros(acc_scr.shape, F32)
    carry_scr[...] = jnp.zeros(carry_scr.shape, F32)
    row = lax.broadcasted_iota(jnp.int32, (tq, tq), 0)
    col = lax.broadcasted_iota(jnp.int32, (tq, tq), 1)

    def body(i, _):
        kt = c - i
        k0 = pl.multiple_of(kt * tq, tq)
        k = k_ref[0, 0, pl.ds(k0, tq), :]
        v = v_ref[0, 0, pl.ds(k0, tq), :]
        z = _dot_nt(q, k)
        sp = jnp.maximum(z, 0.0) + jnp.log(1.0 + jnp.exp(-jnp.abs(z)))
        mask = (col + k0) < (row + c * tq)
        lognb = jnp.where(mask, -sp, 0.0)
        hi, lo = _split2(lognb)
        cs = _dot(hi, uu_ref[...]) + _dot(lo, uu_ref[...])
        arg = (z - sp) + cs[:, :tq] + carry_scr[...]
        w = jnp.where(mask, jnp.exp(arg), 0.0)
        acc_scr[...] += _dot(w.astype(BF16), v)
        carry_scr[...] += cs[:, tq:]
        return 0

    lax.fori_loop(0, c + 1, body, 0)
    out_ref[0, 0] = acc_scr[...]


def _sb_consts(tq):
    uu = np.zeros((tq, 2 * tq), np.float32)
    i = np.arange(tq)
    uu[:, :tq] = (i[:, None] > i[None, :]).astype(np.float32)
    uu[:, tq:] = 1.0
    return jnp.asarray(uu, BF16)


def _sb(q, k, v, tq):
    b, h, s, hd = q.shape
    return pl.pallas_call(
        _sb_kernel,
        grid=(b, h, s // tq),
        in_specs=[pl.BlockSpec((1, 1, tq, hd), lambda bi, hi, ci: (bi, hi, ci, 0)),
                  pl.BlockSpec((1, 1, s, hd), lambda bi, hi, ci: (bi, hi, 0, 0)),
                  pl.BlockSpec((1, 1, s, hd), lambda bi, hi, ci: (bi, hi, 0, 0)),
                  pl.BlockSpec((tq, 2 * tq), lambda bi, hi, ci: (0, 0))],
        out_specs=pl.BlockSpec((1, 1, tq, hd), lambda bi, hi, ci: (bi, hi, ci, 0)),
        out_shape=jax.ShapeDtypeStruct((b, h, s, hd), F32),
        scratch_shapes=[pltpu.VMEM((tq, hd), F32), pltpu.VMEM((tq, tq), F32)],
        compiler_params=_cparams(("parallel", "parallel", "arbitrary")),
        name="sb",
    )(q, k, v, _sb_consts(tq))


def _nsa_prep_kernel(kv_ref, gk_ref, gv_ref, pos_ref, w1k_ref, w2k_ref, w1v_ref, w2v_ref, kn_ref,
                     ct_ref, s1_ref, s2_ref, kc_ref, vc_ref, ks_ref, vs_ref, kw_ref, vw_ref):
    hd = HEAD_DIM
    ncp = gk_ref.shape[1]
    half = gk_ref.shape[2]
    rowi = lax.broadcasted_iota(jnp.int32, (ncp, w1k_ref.shape[1]), 0)

    def compress(g_ref, pos_t, pos_b, w1_ref, w2_ref):
        g = g_ref[0]
        top = _dot((g + pos_t).astype(BF16), w1_ref[0:half, :])
        bot = _dot((g + pos_b).astype(BF16), w1_ref[half:2 * half, :])
        bot = jnp.where(rowi < ncp - 1, pltpu.roll(bot, ncp - 1, 0), 0.0)
        return _dot(_gelu_tanh(top + bot).astype(BF16), w2_ref[...])

    kc = compress(gk_ref, pos_ref[0:1, :], pos_ref[1:2, :], w1k_ref, w2k_ref)
    kc_ref[0] = _rms(kc, kn_ref[0:1, :]).astype(BF16)
    vc_ref[0] = compress(gv_ref, pos_ref[2:3, :], pos_ref[3:4, :], w1v_ref, w2v_ref).astype(BF16)

    ct, s1, s2 = ct_ref[0], s1_ref[0], s2_ref[0]
    k_s = kv_ref[0, :, 2 * hd:3 * hd]
    ks_ref[0] = _rope64(_rms(k_s, kn_ref[1:2, :]), ct, s1, s2).astype(BF16)
    vs_ref[0] = kv_ref[0, :, 3 * hd:4 * hd].astype(BF16)
    k_w = kv_ref[0, :, 4 * hd:5 * hd]
    kw_ref[0] = _rope64(_rms(k_w, kn_ref[2:3, :]), ct, s1, s2).astype(BF16)
    vw_ref[0] = kv_ref[0, :, 5 * hd:6 * hd].astype(BF16)


def _nsa_prep(kv, gk, gv, pos4, w1k, w2k, w1v, w2v, k_norm, ct, s1, s2):
    b, s, _ = kv.shape
    ncp, half = gk.shape[1], gk.shape[2]
    hid = w1k.shape[1]
    hd = HEAD_DIM
    full3 = lambda shp: pl.BlockSpec((1,) + shp, lambda i: (i, 0, 0))
    const2 = lambda shp: pl.BlockSpec(shp, lambda i: (0, 0))
    return pl.pallas_call(
        _nsa_prep_kernel,
        grid=(b,),
        in_specs=[full3((s, 6 * hd)), full3((ncp, half)), full3((ncp, half)), const2((4, half)),
                  const2((2 * half, hid)), const2((hid, hd)), const2((2 * half, hid)), const2((hid, hd)),
                  const2((3, hd)), full3((s, hd)), full3((s, hd)), full3((s, hd))],
        out_specs=[full3((ncp, hd)), full3((ncp, hd)), full3((s, hd)), full3((s, hd)), full3((s, hd)),
                   full3((s, hd))],
        out_shape=[jax.ShapeDtypeStruct((b, ncp, hd), BF16), jax.ShapeDtypeStruct((b, ncp, hd), BF16)]
                  + [jax.ShapeDtypeStruct((b, s, hd), BF16)] * 4,
        compiler_params=_cparams(("parallel",)),
        name="nsa_prep",
    )(kv, gk, gv, pos4, w1k, w2k, w1v, w2v, k_norm, ct, s1, s2)


def _nsa_attn_kernel(q_ref, small_ref, ct_ref, s1_ref, s2_ref, qn_ref, kc_ref, vc_ref, ks_ref, vs_ref,
                     kw_ref, vw_ref, ot_ref, e3_ref, out_ref, imp_scr, m_scr, l_scr, acc_scr):
    nh, hd = NSA_HEADS, HEAD_DIM
    tq = q_ref.shape[1]
    rows = nh * tq
    ncp = kc_ref.shape[1]
    ns = ot_ref.shape[0]
    tk = e3_ref.shape[2]
    top_n = min(NSA_TOPK, ns)
    scale = HEAD_DIM ** -0.5
    c = pl.program_id(1)
    t0 = c * tq

    q = q_ref[0]
    ct, s1, s2 = ct_ref[0], s1_ref[0], s2_ref[0]
    qn_l, qr_l = [], []
    for h in range(nh):
        qn = _rms(q[:, h * hd:(h + 1) * hd], qn_ref[...])
        qn_l.append((qn * scale).astype(BF16))
        qr_l.append((_rope64(qn, ct, s1, s2) * scale).astype(BF16))
    qn_all = jnp.concatenate(qn_l, axis=0)
    qr_all = jnp.concatenate(qr_l, axis=0)

    def tvec(shape):
        return t0 + (lax.broadcasted_iota(jnp.int32, shape, 0) & (tq - 1))

    s = _dot_nt(qn_all, kc_ref[0])
    cmp_end = lax.broadcasted_iota(jnp.int32, (rows, ncp), 1) * NSA_CMP_STRIDE + (NSA_CMP_LEN - 1)
    mask = cmp_end <= tvec((rows, ncp))
    s = jnp.where(mask, s, NEG_INF)
    p = jnp.where(mask, jnp.exp(s - jnp.max(s, axis=-1, keepdims=True)), 0.0)
    p = p / jnp.maximum(jnp.sum(p, axis=-1, keepdims=True), 1e-30)
    o_c = _dot(p.astype(BF16), vc_ref[0])

    psum = p[0:tq]
    for h in range(1, nh):
        psum = psum + p[h * tq:(h + 1) * tq]
    ph, plo = _split2(psum)
    imp = _dot_nt(ot_ref[...], ph) + _dot_nt(ot_ref[...], plo)
    jb = lax.broadcasted_iota(jnp.int32, (ns, tq), 0)
    tl = t0 + lax.broadcasted_iota(jnp.int32, (ns, tq), 1)
    forced = (jb == (tl // NSA_SLC_LEN)) | (jb == 0)
    imp = jnp.where(forced, FORCED, jnp.where(jb * NSA_SLC_LEN <= tl, imp, NEG_INF))
    imp_scr[...] = imp
    cnt = jnp.zeros((ns, tq), F32)
    for i in range(ns):
        bi = imp_scr[i:i + 1, :]
        ge = jnp.where(bi >= imp, 1.0, 0.0)
        gt = jnp.where(bi > imp, 1.0, 0.0)
        cnt = cnt + jnp.where(jb > i, ge, gt)
    sel = jnp.where(cnt < top_n, 1.0, 0.0).T.astype(BF16)

    def reset():
        m_scr[...] = jnp.full(m_scr.shape, NEG_INF, F32)
        l_scr[...] = jnp.zeros(l_scr.shape, F32)
        acc_scr[...] = jnp.zeros(acc_scr.shape, F32)

    def attend(sc, msk, v):
        sc = jnp.where(msk, sc, NEG_INF)
        m_old = m_scr[...]
        m_new = jnp.maximum(m_old, jnp.max(sc, axis=-1, keepdims=True))
        alpha = jnp.exp(m_old - m_new)
        pt = jnp.where(msk, jnp.exp(sc - m_new), 0.0)
        l_scr[...] = alpha * l_scr[...] + jnp.sum(pt, axis=-1, keepdims=True)
        acc_scr[...] = alpha * acc_scr[...] + _dot(pt.astype(BF16), v)
        m_scr[...] = m_new

    def finish():
        return acc_scr[...] / jnp.maximum(l_scr[...], 1e-30)

    reset()
    n_tiles = (t0 + tq + tk - 1) // tk

    def sel_body(kt, _):
        k0 = pl.multiple_of(kt * tk, tk)
        sc = _dot_nt(qr_all, ks_ref[0, pl.ds(k0, tk), :])
        picked = _dot(sel, e3_ref[kt])
        picked = jnp.concatenate([picked] * nh, axis=0)
        kpos = k0 + lax.broadcasted_iota(jnp.int32, (rows, tk), 1)
        msk = jnp.where(kpos <= tvec((rows, tk)), picked, 0.0) > 0.5
        attend(sc, msk, vs_ref[0, pl.ds(k0, tk), :])
        return 0

    lax.fori_loop(0, n_tiles, sel_body, 0)
    o_s = finish()

    reset()
    for i in range(NSA_WINDOW // tq + 1):
        kstart = t0 - NSA_WINDOW + i * tq
        k0 = pl.multiple_of(jnp.maximum(kstart, 0), tq)
        sc = _dot_nt(qr_all, kw_ref[0, pl.ds(k0, tq), :])
        kpos = kstart + lax.broadcasted_iota(jnp.int32, (rows, tq), 1)
        tv = tvec((rows, tq))
        msk = (kpos >= jnp.maximum(tv - (NSA_WINDOW - 1), 0)) & (kpos <= tv)
        attend(sc, msk, vw_ref[0, pl.ds(k0, tq), :])
    o_w = finish()

    g = _sigmoid(small_ref[0])
    outs = []
    for h in range(nh):
        r = slice(h * tq, (h + 1) * tq)
        b0 = SMALL_NSA_G + 3 * h
        outs.append(g[:, b0:b0 + 1] * o_c[r] + g[:, b0 + 1:b0 + 2] * o_s[r] + g[:, b0 + 2:b0 + 3] * o_w[r])
    out_ref[0] = jnp.concatenate(outs, axis=1)


def _nsa_consts(s, tk):
    ncp = s // NSA_CMP_STRIDE
    ns = s // NSA_SLC_LEN
    n_cmp = (s - NSA_CMP_LEN) // NSA_CMP_STRIDE + 1
    cmp_start = np.arange(ncp) * NSA_CMP_STRIDE
    slc_start = np.arange(ns) * NSA_SLC_LEN
    overlap = ((cmp_start[:, None] < slc_start[None, :] + NSA_SLC_LEN)
               & (cmp_start[:, None] + NSA_CMP_LEN > slc_start[None, :])).astype(np.float32)
    overlap[n_cmp:] = 0.0
    kidx = np.arange(s)
    e = (kidx[None, :] // NSA_SLC_LEN == np.arange(ns)[:, None]).astype(np.float32)
    e3 = e.reshape(ns, s // tk, tk).transpose(1, 0, 2)
    return jnp.asarray(overlap.T, BF16), jnp.asarray(e3, BF16)


def _nsa_attn(q, small, ct, s1, s2, q_norm, kc, vc, ks, vs, kw, vw, tq, tk):
    b, s, w = q.shape
    hd = HEAD_DIM
    ncp = kc.shape[1]
    ns = s // NSA_SLC_LEN
    ot, e3 = _nsa_consts(s, tk)
    rows = NSA_HEADS * tq
    blk = lambda width: pl.BlockSpec((1, tq, width), lambda bi, ci: (bi, ci, 0))
    per_b = lambda n: pl.BlockSpec((1, n, hd), lambda bi, ci: (bi, 0, 0))
    return pl.pallas_call(
        _nsa_attn_kernel,
        grid=(b, s // tq),
        in_specs=[blk(w), blk(SMALL_W), blk(hd), blk(hd), blk(hd),
                  pl.BlockSpec((1, hd), lambda bi, ci: (0, 0)),
                  per_b(ncp), per_b(ncp), per_b(s), per_b(s), per_b(s), per_b(s),
                  pl.BlockSpec((ns, ncp), lambda bi, ci: (0, 0)),
                  pl.BlockSpec((s // tk, ns, tk), lambda bi, ci: (0, 0, 0))],
        out_specs=blk(w),
        out_shape=jax.ShapeDtypeStruct((b, s, w), F32),
        scratch_shapes=[pltpu.VMEM((ns, tq), F32), pltpu.VMEM((rows, 1), F32), pltpu.VMEM((rows, 1), F32),
                        pltpu.VMEM((rows, hd), F32)],
        compiler_params=_cparams(("parallel", "arbitrary")),
        name="nsa_attn",
    )(q, small, ct, s1, s2, q_norm, kc, vc, ks, vs, kw, vw, ot, e3)


def _inv_unit_lower(l_strict):
    n = l_strict.shape[0]
    eye = (lax.broadcasted_iota(jnp.int32, (n, n), 0) == lax.broadcasted_iota(jnp.int32, (n, n), 1)).astype(F32)
    prod = eye - l_strict
    power = l_strict
    for _ in range(max(int(math.ceil(math.log2(n))) - 1, 0)):
        power = _dot3(power, power)
        prod = prod + _dot3(prod, power)
    return prod


def _gdn_kernel(x_ref, prev_ref, small_ref, z_ref, cw_ref, nega_ref, dtb_ref, on_ref, tril_ref, out_ref, st_ref):
    nh, hd, ck = GDN_HEADS, HEAD_DIM, GDN_CHUNK
    r = x_ref.shape[1]
    width = x_ref.shape[2]
    i_blk = pl.program_id(1)

    @pl.when(i_blk == 0)
    def _():
        st_ref[...] = jnp.zeros(st_ref.shape, F32)

    x = x_ref[0]
    prev = jnp.where(i_blk > 0, prev_ref[0], 0.0)
    r8 = lax.broadcasted_iota(jnp.int32, (SUBLANES, width), 0)
    conv = x * cw_ref[GDN_CONV - 1:GDN_CONV, :]
    for back in range(1, GDN_CONV):
        xs = pltpu.roll(x, back, 0)
        head = jnp.where(r8 < back, pltpu.roll(prev, back, 0), xs[0:SUBLANES])
        xs = jnp.concatenate([head, xs[SUBLANES:]], axis=0)
        conv = conv + xs * cw_ref[GDN_CONV - 1 - back:GDN_CONV - back, :]
    act = conv * _sigmoid(conv)

    sm = small_ref[0]
    beta_all = _sigmoid(sm)
    g_all = nega_ref[...] * _softplus(sm + dtb_ref[...])
    zf = z_ref[0]
    tril = tril_ref[...]
    ri = lax.broadcasted_iota(jnp.int32, (ck, ck), 0)
    ci = lax.broadcasted_iota(jnp.int32, (ck, ck), 1)
    lower = ri >= ci
    strict = ri > ci

    out_chunks = []
    for cidx in range(r // ck):
        rs = slice(cidx * ck, (cidx + 1) * ck)
        g3 = _split3(g_all[rs])
        gcum = _dot(tril, g3[0]) + (_dot(tril, g3[1]) + _dot(tril, g3[2]))
        gcum_t = gcum.T
        heads_out = []
        for h in range(nh):
            q = act[rs, h * hd:(h + 1) * hd]
            k = act[rs, GDN_W + h * hd:GDN_W + (h + 1) * hd]
            v = act[rs, 2 * GDN_W + h * hd:2 * GDN_W + (h + 1) * hd]
            q = q * lax.rsqrt(jnp.sum(q * q, axis=-1, keepdims=True) + NORM_EPS) * (HEAD_DIM ** -0.5)
            k = k * lax.rsqrt(jnp.sum(k * k, axis=-1, keepdims=True) + NORM_EPS)
            beta = beta_all[rs, SMALL_GDN_B + h:SMALL_GDN_B + h + 1]
            gc = gcum[:, SMALL_GDN_A + h:SMALL_GDN_A + h + 1]
            gr = gcum_t[SMALL_GDN_A + h:SMALL_GDN_A + h + 1, :]
            g_last = gcum[ck - 1:ck, SMALL_GDN_A + h:SMALL_GDN_A + h + 1]
            decay = jnp.where(lower, jnp.exp(jnp.where(lower, gc - gr, 0.0)), 0.0)
            kb = k * beta
            vb = v * beta
            k16 = k.astype(BF16)
            l_mat = jnp.where(strict, _dot_nt(kb.astype(BF16), k16) * decay, 0.0)
            a_intra = jnp.where(lower, _dot_nt(q.astype(BF16), k16) * decay, 0.0)
            t_inv = _inv_unit_lower(l_mat).astype(BF16)
            e_gc = jnp.exp(gc)
            u = _dot(t_inv, vb.astype(BF16))
            w = _dot(t_inv, (kb * e_gc).astype(BF16))
            q_dec = q * e_gc
            k_dec = k * jnp.exp(g_last - gc)
            state = st_ref[h]
            s16 = state.astype(BF16)
            v_new = u - _dot(w.astype(BF16), s16)
            o = _dot(q_dec.astype(BF16), s16) + _dot(a_intra.astype(BF16), v_new.astype(BF16))
            st_ref[h] = state * jnp.exp(g_last) + _dot(k_dec.T.astype(BF16), v_new.astype(BF16))
            zh = zf[rs, h * hd:(h + 1) * hd]
            heads_out.append(_rms(o, on_ref[...]) * (zh * _sigmoid(zh)))
        out_chunks.append(jnp.concatenate(heads_out, axis=1))
    out_ref[0] = jnp.concatenate(out_chunks, axis=0)


def _gdn(qkv, small, z, conv_w, neg_a_row, dtb_row, o_norm, r):
    b, s, width = qkv.shape
    ck = GDN_CHUNK
    tril = jnp.asarray(np.tril(np.ones((ck, ck), np.float32)), BF16)
    nprev = r // SUBLANES
    return pl.pallas_call(
        _gdn_kernel,
        grid=(b, s // r),
        in_specs=[pl.BlockSpec((1, r, width), lambda bi, i: (bi, i, 0)),
                  pl.BlockSpec((1, SUBLANES, width), lambda bi, i: (bi, jnp.maximum(i * nprev - 1, 0), 0)),
                  pl.BlockSpec((1, r, SMALL_W), lambda bi, i: (bi, i, 0)),
                  pl.BlockSpec((1, r, GDN_W), lambda bi, i: (bi, i, 0)),
                  pl.BlockSpec((GDN_CONV, width), lambda bi, i: (0, 0)),
                  pl.BlockSpec((1, SMALL_W), lambda bi, i: (0, 0)),
                  pl.BlockSpec((1, SMALL_W), lambda bi, i: (0, 0)),
                  pl.BlockSpec((1, HEAD_DIM), lambda bi, i: (0, 0)),
                  pl.BlockSpec((ck, ck), lambda bi, i: (0, 0))],
        out_specs=pl.BlockSpec((1, r, GDN_W), lambda bi, i: (bi, i, 0)),
        out_shape=jax.ShapeDtypeStruct((b, s, GDN_W), F32),
        scratch_shapes=[pltpu.VMEM((GDN_HEADS, HEAD_DIM, HEAD_DIM), F32)],
        compiler_params=_cparams(("parallel", "arbitrary")),
        name="gdn",
    )(qkv, qkv, small, z, conv_w, neg_a_row, dtb_row, o_norm, tril)


def _rope_tables(positions):
    half = ROT_DIM // 2
    inv_freq = ROPE_THETA ** (-jnp.arange(half, dtype=F32) / half)
    ang = positions.astype(F32)[..., None] * inv_freq
    cos, sin = jnp.cos(ang), jnp.sin(ang)
    rest = HEAD_DIM - ROT_DIM
    ones = jnp.ones(cos.shape[:-1] + (rest,), F32)
    zeros = jnp.zeros(cos.shape[:-1] + (rest,), F32)
    zh = jnp.zeros_like(sin)
    ct = jnp.concatenate([cos, cos, ones], axis=-1)
    s1 = jnp.concatenate([-sin, zh, zeros], axis=-1)
    s2 = jnp.concatenate([zh, sin, zeros], axis=-1)
    return ct, s1, s2


def _row(vec, offset):
    return jnp.zeros((1, SMALL_W), F32).at[0, offset:offset + vec.shape[0]].set(vec.astype(F32))


def _layer(x2, b, s, ct, s1, s2, attn_norm, w_in, nsa_q_norm, nsa_k_norm, nsa_cmp_pos, ck_w1, ck_w2, cv_w1,
           cv_w2, lam_re, lam_im, log_dt, b_re, b_im, c_re, c_im, s5_d, s5_glu_w, conv_w, a_log, dt_bias,
           o_norm, w_branch, w_out, ffn_norm, w_gate, w_up, w_down, tiles):
    hd = HEAD_DIM
    w_a, w_g = _arrange_w_in(w_in)
    nsa_q, nsa_kv, s5_u, gdn_qkv, gdn_z, sb_qkv, small = _inproj(x2, attn_norm.reshape(1, -1), w_a, tiles['tm'])
    r3 = lambda a: a.reshape(b, s, a.shape[-1])

    kv3 = r3(nsa_kv)
    grp = NSA_CMP_STRIDE * hd
    gk = kv3[:, :, 0:hd].reshape(b, s // NSA_CMP_STRIDE, grp)
    gv = kv3[:, :, hd:2 * hd].reshape(b, s // NSA_CMP_STRIDE, grp)
    pos4 = nsa_cmp_pos.reshape(4, grp)
    kc, vc, ks, vs, kw, vw = _nsa_prep(kv3, gk, gv, pos4, ck_w1.astype(BF16), ck_w2.astype(BF16),
                                       cv_w1.astype(BF16), cv_w2.astype(BF16), nsa_k_norm, ct, s1, s2)
    o_nsa = _nsa_attn(r3(nsa_q), r3(small), ct, s1, s2, nsa_q_norm.reshape(1, hd), kc, vc, ks, vs, kw, vw,
                      Q_BLOCK, tiles['nsa_tk'])

    wb, are, aim, cm = _s5_params(lam_re, lam_im, log_dt, b_re, b_im, c_re, c_im)
    u_tm = r3(s5_u).transpose(1, 0, 2).reshape(s * b, S5_W)
    o_s5 = _s5(u_tm, b, wb, are, aim, cm, s5_d.reshape(1, S5_W), s5_glu_w.astype(BF16), tiles['s5_steps'])
    o_s5 = o_s5.reshape(s, b, S5_W).transpose(1, 0, 2).reshape(b * s, S5_W)

    o_gdn = _gdn(r3(gdn_qkv), r3(small), r3(gdn_z), conv_w, _row(-jnp.exp(a_log), SMALL_GDN_A),
                 _row(dt_bias, SMALL_GDN_A), o_norm.reshape(1, hd), tiles['gdn_rows'])

    qkv5 = sb_qkv.reshape(b, s, 3, SB_HEADS, hd).transpose(2, 0, 3, 1, 4)
    o_sb = _sb(qkv5[0] * jnp.asarray(hd ** -0.5, BF16), qkv5[1], qkv5[2], Q_BLOCK)
    o_sb = o_sb.transpose(0, 2, 1, 3).reshape(b * s, SB_W)

    x2 = _merge(x2, attn_norm.reshape(1, -1), w_g, (o_nsa.reshape(b * s, NSA_W), o_s5, o_gdn.reshape(b * s, GDN_W),
                                                    o_sb), w_branch.astype(BF16), w_out.astype(BF16), tiles['tm_merge'])
    return _ffn(x2, ffn_norm.reshape(1, -1), w_gate.astype(BF16), w_up.astype(BF16), w_down.astype(BF16),
                tiles['tm_ffn'], tiles['tf'])


def _tiles(b, s):
    t = b * s
    return dict(tm=min(512, t), tm_merge=min(256, t), tm_ffn=min(512, t), tf=D_FF // 2,
                nsa_tk=min(512, s), s5_steps=min(64, s), gdn_rows=min(256, s))


def kernel(x, positions, attn_norm, w_in, nsa_q_norm, nsa_k_norm, nsa_cmp_pos, nsa_cmp_k_w1, nsa_cmp_k_w2, nsa_cmp_v_w1, nsa_cmp_v_w2, s5_lam_re, s5_lam_im, s5_log_dt, s5_b_re, s5_b_im, s5_c_re, s5_c_im, s5_d, s5_glu_w, gdn_conv_w, gdn_a_log, gdn_dt_bias, gdn_o_norm, w_branch, w_out, ffn_norm, w_gate, w_up, w_down):
    b, s, d = x.shape
    ct, s1, s2 = _rope_tables(positions)
    tiles = _tiles(b, s)
    x2 = x.reshape(b * s, d)
    per_layer = (attn_norm, w_in, nsa_q_norm, nsa_k_norm, nsa_cmp_pos, nsa_cmp_k_w1, nsa_cmp_k_w2, nsa_cmp_v_w1,
                 nsa_cmp_v_w2, s5_lam_re, s5_lam_im, s5_log_dt, s5_b_re, s5_b_im, s5_c_re, s5_c_im, s5_d, s5_glu_w,
                 gdn_conv_w, gdn_a_log, gdn_dt_bias, gdn_o_norm, w_branch, w_out, ffn_norm, w_gate, w_up, w_down)
    for l in range(attn_norm.shape[0]):
        x2 = _layer(x2, b, s, ct, s1, s2, *[p[l] for p in per_layer], tiles)
    return x2.reshape(b, s, d)
```

```python
import functools
import math

import numpy as np
import jax
import jax.numpy as jnp
from jax import lax
from jax.experimental import pallas as pl
from jax.experimental.pallas import tpu as pltpu

F32 = jnp.float32
BF16 = jnp.bfloat16

D_MODEL = 1024
HEAD_DIM = 64
ROT_DIM = HEAD_DIM // 4
ROPE_THETA = 500000.0
NORM_EPS = 1e-6
NEG_INF = -1e30
FORCED = 1e9

NSA_HEADS = 4
NSA_CMP_LEN = 32
NSA_CMP_STRIDE = 16
NSA_SLC_LEN = 64
NSA_TOPK = 16
NSA_WINDOW = 512
NSA_W = NSA_HEADS * HEAD_DIM

S5_GROUPS = 16
S5_GROUP_CH = 16
S5_W = S5_GROUPS * S5_GROUP_CH
S5_STATE = 64
S5_LANES = S5_GROUPS * S5_STATE

GDN_HEADS = 4
GDN_CONV = 4
GDN_CHUNK = 64
GDN_W = GDN_HEADS * HEAD_DIM

SB_HEADS = 4
SB_W = SB_HEADS * HEAD_DIM

N_BRANCH = 4
D_FF = 256 * math.ceil(8 * D_MODEL / (3 * 256))

LANES = 128
SUBLANES = 8

SMALL_NSA_G = 0
SMALL_GDN_A = 12
SMALL_GDN_B = 16
SMALL_W = LANES

IN_SPLITS = (NSA_W, 6 * HEAD_DIM, 3 * NSA_HEADS, S5_W, 3 * GDN_W, GDN_HEADS, GDN_HEADS, GDN_W, 3 * SB_W,
             N_BRANCH * D_MODEL)
IN_OFFSETS = tuple(int(v) for v in np.cumsum((0,) + IN_SPLITS))

LOG2E = 1.4426950408889634
LN2 = 0.6931471805599453

VMEM_LIMIT = 56 * 1024 * 1024


def _cparams(sem):
    return pltpu.CompilerParams(dimension_semantics=sem, vmem_limit_bytes=VMEM_LIMIT)


def _rms(x, gain):
    return x * lax.rsqrt(jnp.mean(x * x, axis=-1, keepdims=True) + NORM_EPS) * gain


def _gelu_tanh(x):
    return 0.5 * x * (1.0 + jnp.tanh(math.sqrt(2.0 / math.pi) * (x + 0.044715 * (x * x * x))))


def _sigmoid(x):
    return 1.0 / (1.0 + jnp.exp(-x))


def _softplus(x):
    return jnp.maximum(x, 0.0) + jnp.log(1.0 + jnp.exp(-jnp.abs(x)))


def _dot(a, b):
    return jnp.dot(a, b, preferred_element_type=F32)


def _dot_nt(a, b):
    return lax.dot_general(a, b, (((1,), (1,)), ((), ())), preferred_element_type=F32)


def _split2(x):
    hi = x.astype(BF16)
    lo = (x - hi.astype(F32)).astype(BF16)
    return hi, lo


def _split3(x):
    hi = x.astype(BF16)
    r = x - hi.astype(F32)
    mid = r.astype(BF16)
    lo = (r - mid.astype(F32)).astype(BF16)
    return hi, mid, lo


def _rope64(xn, ct, s1, s2):
    half = ROT_DIM // 2
    left = jnp.concatenate([xn[:, half:], xn[:, :half]], axis=1)
    right = jnp.concatenate([xn[:, HEAD_DIM - half:], xn[:, :HEAD_DIM - half]], axis=1)
    return xn * ct + left * s1 + right * s2


INPROJ_WIDTHS = (NSA_W, 6 * HEAD_DIM, S5_W, 3 * GDN_W, GDN_W, 3 * SB_W, SMALL_W)
INPROJ_DTYPES = (F32, F32, F32, F32, F32, BF16, F32)


def _inproj_kernel(x_ref, g_ref, w_ref, *out_refs):
    h = _rms(x_ref[...], g_ref[...]).astype(BF16)
    off = 0
    for o_ref, width in zip(out_refs, INPROJ_WIDTHS):
        o_ref[...] = _dot(h, w_ref[:, off:off + width]).astype(o_ref.dtype)
        off += width


def _inproj(x2, gain, w_a, tm):
    t, d = x2.shape
    n = w_a.shape[1]
    return pl.pallas_call(
        _inproj_kernel,
        grid=(t // tm,),
        in_specs=[pl.BlockSpec((tm, d), lambda i: (i, 0)),
                  pl.BlockSpec((1, d), lambda i: (0, 0)),
                  pl.BlockSpec((d, n), lambda i: (0, 0))],
        out_specs=[pl.BlockSpec((tm, w), lambda i: (i, 0)) for w in INPROJ_WIDTHS],
        out_shape=[jax.ShapeDtypeStruct((t, w), dt) for w, dt in zip(INPROJ_WIDTHS, INPROJ_DTYPES)],
        compiler_params=_cparams(("parallel",)),
        name="inproj",
    )(x2, gain, w_a)


def _arrange_w_in(w_in_l):
    o = IN_OFFSETS
    small = jnp.concatenate([w_in_l[:, o[2]:o[3]], w_in_l[:, o[5]:o[6]], w_in_l[:, o[6]:o[7]]], axis=1)
    small = jnp.pad(small, ((0, 0), (0, SMALL_W - small.shape[1])))
    cols = [w_in_l[:, o[0]:o[1]], w_in_l[:, o[1]:o[2]], w_in_l[:, o[3]:o[4]], w_in_l[:, o[4]:o[5]],
            w_in_l[:, o[7]:o[8]], w_in_l[:, o[8]:o[9]], small]
    return jnp.concatenate(cols, axis=1).astype(BF16), w_in_l[:, o[9]:o[10]].astype(BF16)


def _merge_kernel(x_ref, g_ref, wg_ref, o1_ref, o2_ref, o3_ref, o4_ref, p_ref, wo_ref, out_ref):
    x = x_ref[...]
    d = x.shape[1]
    h = _rms(x, g_ref[...]).astype(BF16)
    merged = jnp.zeros(x.shape, F32)
    for m, o_ref in enumerate((o1_ref, o2_ref, o3_ref, o4_ref)):
        gate = _sigmoid(_dot(h, wg_ref[:, m * d:(m + 1) * d]))
        merged = merged + gate * _dot(o_ref[...].astype(BF16), p_ref[m])
    out_ref[...] = x + _dot(merged.astype(BF16), wo_ref[...])


def _merge(x2, gain, w_g, outs, w_branch, w_out, tm):
    t, d = x2.shape
    bw = outs[0].shape[1]
    return pl.pallas_call(
        _merge_kernel,
        grid=(t // tm,),
        in_specs=[pl.BlockSpec((tm, d), lambda i: (i, 0)),
                  pl.BlockSpec((1, d), lambda i: (0, 0)),
                  pl.BlockSpec((d, N_BRANCH * d), lambda i: (0, 0))]
                 + [pl.BlockSpec((tm, bw), lambda i: (i, 0)) for _ in range(N_BRANCH)]
                 + [pl.BlockSpec((N_BRANCH, bw, d), lambda i: (0, 0, 0)),
                    pl.BlockSpec((d, d), lambda i: (0, 0))],
        out_specs=pl.BlockSpec((tm, d), lambda i: (i, 0)),
        out_shape=jax.ShapeDtypeStruct((t, d), F32),
        compiler_params=_cparams(("parallel",)),
        name="merge",
    )(x2, gain, w_g, *outs, w_branch, w_out)


def _ffn_kernel(x_ref, g_ref, wg_ref, wu_ref, wd_ref, out_ref, h_scr, acc_scr):
    j = pl.program_id(1)

    @pl.when(j == 0)
    def _():
        h_scr[...] = _rms(x_ref[...], g_ref[...]).astype(BF16)
        acc_scr[...] = jnp.zeros(acc_scr.shape, F32)

    h = h_scr[...]
    a = _dot(h, wg_ref[...])
    u = _dot(h, wu_ref[...])
    act = (a * _sigmoid(a) * u).astype(BF16)
    acc_scr[...] += _dot(act, wd_ref[...])

    @pl.when(j == pl.num_programs(1) - 1)
    def _():
        out_ref[...] = x_ref[...] + acc_scr[...]


def _ffn(x2, gain, w_gate, w_up, w_down, tm, tf):
    t, d = x2.shape
    f = w_gate.shape[1]
    return pl.pallas_call(
        _ffn_kernel,
        grid=(t // tm, f // tf),
        in_specs=[pl.BlockSpec((tm, d), lambda i, j: (i, 0)),
                  pl.BlockSpec((1, d), lambda i, j: (0, 0)),
                  pl.BlockSpec((d, tf), lambda i, j: (0, j)),
                  pl.BlockSpec((d, tf), lambda i, j: (0, j)),
                  pl.BlockSpec((tf, d), lambda i, j: (j, 0))],
        out_specs=pl.BlockSpec((tm, d), lambda i, j: (i, 0)),
        out_shape=jax.ShapeDtypeStruct((t, d), F32),
        scratch_shapes=[pltpu.VMEM((tm, d), BF16), pltpu.VMEM((tm, d), F32)],
        compiler_params=_cparams(("parallel", "arbitrary")),
        name="ffn",
    )(x2, gain, w_gate, w_up, w_down)


def _s5_kernel(u_ref, wb_ref, are_ref, aim_ref, cm_ref, d_ref, glu_ref, out_ref, st_ref, bu_ref):
    nb = st_ref.shape[0]
    p = are_ref.shape[1]
    steps = u_ref.shape[0] // nb

    @pl.when(pl.program_id(0) == 0)
    def _():
        st_ref[...] = jnp.zeros(st_ref.shape, F32)

    u = u_ref[...]
    bu_ref[...] = _dot(u.astype(BF16), wb_ref[...])
    are = jnp.broadcast_to(are_ref[...], (nb, p))
    aim = jnp.broadcast_to(aim_ref[...], (nb, p))

    def body(t, carry):
        xr, xi = carry
        r0 = pl.multiple_of(t * nb, nb)
        br = bu_ref[pl.ds(r0, nb), 0:p]
        bi = bu_ref[pl.ds(r0, nb), p:2 * p]
        nr = are * xr - aim * xi + br
        ni = are * xi + aim * xr + bi
        bu_ref[pl.ds(r0, nb), 0:p] = nr
        bu_ref[pl.ds(r0, nb), p:2 * p] = ni
        return nr, ni

    xr, xi = lax.fori_loop(0, steps, body, (st_ref[:, 0:p], st_ref[:, p:2 * p]))
    st_ref[:, 0:p] = xr
    st_ref[:, p:2 * p] = xi

    y = _dot(bu_ref[...].astype(BF16), cm_ref[...]) + d_ref[...] * u
    y = _gelu_tanh(y)
    z = _dot(y.astype(BF16), glu_ref[...])
    w = z.shape[1] // 2
    out_ref[...] = z[:, :w] * _sigmoid(z[:, w:])


def _s5_params(lam_re, lam_im, log_dt, b_re, b_im, c_re, c_im):
    dt = jnp.exp(log_dt)[:, None]
    mag = jnp.exp(lam_re * dt)
    ab_re, ab_im = mag * jnp.cos(lam_im * dt), mag * jnp.sin(lam_im * dt)
    den = lam_re * lam_re + lam_im * lam_im
    f_re = ((ab_re - 1.0) * lam_re + ab_im * lam_im) / den
    f_im = (ab_im * lam_re - (ab_re - 1.0) * lam_im) / den
    bb_re = f_re[..., None] * b_re - f_im[..., None] * b_im
    bb_im = f_re[..., None] * b_im + f_im[..., None] * b_re
    eye = jnp.eye(S5_GROUPS, dtype=F32)
    wb_re = jnp.einsum('gpc,gh->gchp', bb_re, eye).reshape(S5_W, S5_LANES)
    wb_im = jnp.einsum('gpc,gh->gchp', bb_im, eye).reshape(S5_W, S5_LANES)
    wb = jnp.concatenate([wb_re, wb_im], axis=1).astype(BF16)
    cm_re = jnp.einsum('gcp,gh->hpgc', c_re, eye).reshape(S5_LANES, S5_W)
    cm_im = jnp.einsum('gcp,gh->hpgc', c_im, eye).reshape(S5_LANES, S5_W)
    cm = jnp.concatenate([cm_re, -cm_im], axis=0).astype(BF16)
    return wb, ab_re.reshape(1, S5_LANES), ab_im.reshape(1, S5_LANES), cm


def _s5(u_tm, nb, wb, are, aim, cm, d_skip, glu_w, steps):
    rows, w = u_tm.shape
    r = steps * nb
    p = are.shape[1]
    return pl.pallas_call(
        _s5_kernel,
        grid=(rows // r,),
        in_specs=[pl.BlockSpec((r, w), lambda i: (i, 0)),
                  pl.BlockSpec((w, 2 * p), lambda i: (0, 0)),
                  pl.BlockSpec((1, p), lambda i: (0, 0)),
                  pl.BlockSpec((1, p), lambda i: (0, 0)),
                  pl.BlockSpec((2 * p, w), lambda i: (0, 0)),
                  pl.BlockSpec((1, w), lambda i: (0, 0)),
                  pl.BlockSpec((w, 2 * w), lambda i: (0, 0))],
        out_specs=pl.BlockSpec((r, w), lambda i: (i, 0)),
        out_shape=jax.ShapeDtypeStruct((rows, w), F32),
        scratch_shapes=[pltpu.VMEM((nb, 2 * p), F32), pltpu.VMEM((r, 2 * p), F32)],
        compiler_params=_cparams(("arbitrary",)),
        name="s5",
    )(u_tm, wb, are, aim, cm, d_skip, glu_w)


def _sb_kernel(q_ref, k_ref, v_ref, u_ref, out_ref, q_scr, k_scr, v_scr, acc_scr, carry_scr):
    nh, hd = SB_HEADS, HEAD_DIM
    c = pl.program_id(1)
    tq = q_ref.shape[1]

    @pl.when(c == 0)
    def _():
        for h in range(nh):
            k_scr[h] = k_ref[0, :, h * hd:(h + 1) * hd]
            v_scr[h] = v_ref[0, :, h * hd:(h + 1) * hd]

    q = q_ref[0]
    for h in range(nh):
        q_scr[h] = q[:, h * hd:(h + 1) * hd] * jnp.asarray(hd ** -0.5, BF16)
    acc_scr[...] = jnp.zeros(acc_scr.shape, F32)
    carry_scr[...] = jnp.zeros(carry_scr.shape, F32)

    def chunk(k0, diagonal):
        if diagonal:
            strict = (lax.broadcasted_iota(jnp.int32, (tq, tq), 1) < lax.broadcasted_iota(jnp.int32, (tq, tq), 0))
        zs, sps, parts = [], [], []
        for h in range(nh):
            z = _dot_nt(q_scr[h], k_scr[h, pl.ds(k0, tq), :])
            sp = jnp.maximum(z, 0.0) + LN2 * jnp.log2(1.0 + jnp.exp2(jnp.abs(z) * (-LOG2E)))
            if diagonal:
                sp = jnp.where(strict, sp, 0.0)
            zs.append(z)
            sps.append(sp)
            parts.append(_split2(sp))
        befores = [_dot(hi, u_ref[...]) + _dot(lo, u_ref[...]) for hi, lo in parts]
        for h in range(nh):
            w = jnp.exp((zs[h] - sps[h]) - befores[h] + carry_scr[h])
            if diagonal:
                w = jnp.where(strict, w, 0.0)
            acc_scr[h] += _dot(w.astype(BF16), v_scr[h, pl.ds(k0, tq), :])
            carry_scr[h] -= befores[h][:, 0:1] + sps[h][:, 0:1]

    chunk(pl.multiple_of(c * tq, tq), True)

    def body(i, _):
        chunk(pl.multiple_of((c - i) * tq, tq), False)
        return 0

    lax.fori_loop(1, c + 1, body, 0)
    out_ref[0] = jnp.concatenate([acc_scr[h] for h in range(nh)], axis=1)


def _sb_consts(tq):
    i = np.arange(tq)
    return jnp.asarray((i[:, None] > i[None, :]).astype(np.float32), BF16)


def _sb(qkv, tq):
    b, s, w3 = qkv.shape
    w = w3 // 3
    nh, hd = SB_HEADS, HEAD_DIM
    return pl.pallas_call(
        _sb_kernel,
        grid=(b, s // tq),
        in_specs=[pl.BlockSpec((1, tq, w), lambda bi, ci: (bi, ci, 0)),
                  pl.BlockSpec((1, s, w), lambda bi, ci: (bi, 0, 1)),
                  pl.BlockSpec((1, s, w), lambda bi, ci: (bi, 0, 2)),
                  pl.BlockSpec((tq, tq), lambda bi, ci: (0, 0))],
        out_specs=pl.BlockSpec((1, tq, w), lambda bi, ci: (bi, ci, 0)),
        out_shape=jax.ShapeDtypeStruct((b, s, w), F32),
        scratch_shapes=[pltpu.VMEM((nh, tq, hd), BF16), pltpu.VMEM((nh, s, hd), BF16),
                        pltpu.VMEM((nh, s, hd), BF16), pltpu.VMEM((nh, tq, hd), F32),
                        pltpu.VMEM((nh, tq, 1), F32)],
        compiler_params=_cparams(("parallel", "arbitrary")),
        name="sb",
    )(qkv, qkv, qkv, _sb_consts(tq))


def _nsa_prep_kernel(kv_ref, gk_ref, gv_ref, pos_ref, w1k_ref, w2k_ref, w1v_ref, w2v_ref, kn_ref,
                     ct_ref, s1_ref, s2_ref, kc_ref, vc_ref, ks_ref, vs_ref, kw_ref, vw_ref):
    hd = HEAD_DIM
    ncp = gk_ref.shape[1]
    half = gk_ref.shape[2]
    rowi = lax.broadcasted_iota(jnp.int32, (ncp, w1k_ref.shape[1]), 0)

    def compress(g_ref, pos_t, pos_b, w1_ref, w2_ref):
        g = g_ref[0]
        top = _dot((g + pos_t).astype(BF16), w1_ref[0:half, :])
        bot = _dot((g + pos_b).astype(BF16), w1_ref[half:2 * half, :])
        bot = jnp.where(rowi < ncp - 1, pltpu.roll(bot, ncp - 1, 0), 0.0)
        return _dot(_gelu_tanh(top + bot).astype(BF16), w2_ref[...])

    kc = compress(gk_ref, pos_ref[0:1, :], pos_ref[1:2, :], w1k_ref, w2k_ref)
    kc_ref[0] = _rms(kc, kn_ref[0:1, :]).astype(BF16)
    vc_ref[0] = compress(gv_ref, pos_ref[2:3, :], pos_ref[3:4, :], w1v_ref, w2v_ref).astype(BF16)

    ct, s1, s2 = ct_ref[0], s1_ref[0], s2_ref[0]
    ones_col = (lax.broadcasted_iota(jnp.int32, (kv_ref.shape[1], hd), 1) == 0).astype(BF16)
    k_s = kv_ref[0, :, 2 * hd:3 * hd]
    ks_ref[0] = _rope64(_rms(k_s, kn_ref[1:2, :]), ct, s1, s2).astype(BF16)
    vs_ref[0] = jnp.concatenate([kv_ref[0, :, 3 * hd:4 * hd].astype(BF16), ones_col], axis=1)
    k_w = kv_ref[0, :, 4 * hd:5 * hd]
    kw_ref[0] = _rope64(_rms(k_w, kn_ref[2:3, :]), ct, s1, s2).astype(BF16)
    vw_ref[0] = jnp.concatenate([kv_ref[0, :, 5 * hd:6 * hd].astype(BF16), ones_col], axis=1)


def _nsa_prep(kv, gk, gv, pos4, w1k, w2k, w1v, w2v, k_norm, ct, s1, s2):
    b, s, _ = kv.shape
    ncp, half = gk.shape[1], gk.shape[2]
    hid = w1k.shape[1]
    hd = HEAD_DIM
    full3 = lambda shp: pl.BlockSpec((1,) + shp, lambda i: (i, 0, 0))
    const2 = lambda shp: pl.BlockSpec(shp, lambda i: (0, 0))
    return pl.pallas_call(
        _nsa_prep_kernel,
        grid=(b,),
        in_specs=[full3((s, 6 * hd)), full3((ncp, half)), full3((ncp, half)), const2((4, half)),
                  const2((2 * half, hid)), const2((hid, hd)), const2((2 * half, hid)), const2((hid, hd)),
                  const2((3, hd)), full3((s, hd)), full3((s, hd)), full3((s, hd))],
        out_specs=[full3((ncp, hd)), full3((ncp, hd)), full3((s, hd)), full3((s, 2 * hd)), full3((s, hd)),
                   full3((s, 2 * hd))],
        out_shape=[jax.ShapeDtypeStruct((b, ncp, hd), BF16), jax.ShapeDtypeStruct((b, ncp, hd), BF16)]
                  + [jax.ShapeDtypeStruct((b, s, hd), BF16), jax.ShapeDtypeStruct((b, s, 2 * hd), BF16)] * 2,
        compiler_params=_cparams(("parallel",)),
        name="nsa_prep",
    )(kv, gk, gv, pos4, w1k, w2k, w1v, w2v, k_norm, ct, s1, s2)


def _nsa_attn_kernel(q_ref, small_ref, ct_ref, s1_ref, s2_ref, qn_ref, kc_ref, vc_ref, ks_ref, vs_ref,
                     kw_ref, vw_ref, ot_ref, e3_ref, out_ref, imp_scr, qr_scr, m_scr, acc_scr):
    nh, hd = NSA_HEADS, HEAD_DIM
    tq = q_ref.shape[1]
    rows = nh * tq
    ncp = kc_ref.shape[1]
    ns = ot_ref.shape[0]
    tk = e3_ref.shape[2]
    top_n = min(NSA_TOPK, ns)
    scale = HEAD_DIM ** -0.5
    c = pl.program_id(1)
    t0 = c * tq

    q = q_ref[0]
    ct, s1, s2 = ct_ref[0], s1_ref[0], s2_ref[0]
    qn_l, qr_l = [], []
    for h in range(nh):
        qn = _rms(q[:, h * hd:(h + 1) * hd], qn_ref[...])
        qn_l.append((qn * scale).astype(BF16))
        qr_l.append((_rope64(qn, ct, s1, s2) * scale).astype(BF16))
    qn_all = jnp.concatenate(qn_l, axis=0)

    def tvec(shape):
        return t0 + (lax.broadcasted_iota(jnp.int32, shape, 0) & (tq - 1))

    s = _dot_nt(qn_all, kc_ref[0])
    cmp_end = lax.broadcasted_iota(jnp.int32, (rows, ncp), 1) * NSA_CMP_STRIDE + (NSA_CMP_LEN - 1)
    mask = cmp_end <= tvec((rows, ncp))
    s = jnp.where(mask, s, NEG_INF)
    p = jnp.where(mask, jnp.exp(s - jnp.max(s, axis=-1, keepdims=True)), 0.0)
    p = p / jnp.maximum(jnp.sum(p, axis=-1, keepdims=True), 1e-30)
    o_c = _dot(p.astype(BF16), vc_ref[0])

    psum = p[0:tq]
    for h in range(1, nh):
        psum = psum + p[h * tq:(h + 1) * tq]
    ph, plo = _split2(psum)
    imp = _dot_nt(ot_ref[...], ph) + _dot_nt(ot_ref[...], plo)
    jb = lax.broadcasted_iota(jnp.int32, (ns, tq), 0)
    tl = t0 + lax.broadcasted_iota(jnp.int32, (ns, tq), 1)
    forced = (jb == (tl // NSA_SLC_LEN)) | (jb == 0)
    imp = jnp.where(forced, FORCED, jnp.where(jb * NSA_SLC_LEN <= tl, imp, NEG_INF))
    imp_scr[...] = imp
    cnt = jnp.zeros((ns, tq), F32)
    for i in range(ns):
        bi = imp_scr[i:i + 1, :]
        ge = jnp.where(bi >= imp, 1.0, 0.0)
        gt = jnp.where(bi > imp, 1.0, 0.0)
        cnt = cnt + jnp.where(jb > i, ge, gt)
    sel = jnp.where(cnt < top_n, 1.0, 0.0).T.astype(BF16)

    def finish(acc):
        return acc[:, :hd] / jnp.maximum(acc[:, hd:hd + 1], 1e-30)

    m_scr[...] = jnp.full(m_scr.shape, NEG_INF, F32)
    acc_scr[...] = jnp.zeros(acc_scr.shape, F32)
    for h in range(nh):
        qr_scr[h] = qr_l[h]

    def sel_tile(kt, causal):
        k0 = pl.multiple_of(kt * tk, tk)
        bias = (_dot(sel, e3_ref[kt]) - 1.0) * (-NEG_INF)
        if causal:
            kpos = k0 + lax.broadcasted_iota(jnp.int32, (tq, tk), 1)
            bias = jnp.where(kpos <= t0 + lax.broadcasted_iota(jnp.int32, (tq, tk), 0), bias, NEG_INF)
        k = ks_ref[0, pl.ds(k0, tk), :]
        v = vs_ref[0, pl.ds(k0, tk), :]
        scs = [_dot_nt(qr_scr[h], k) + bias for h in range(nh)]
        for h in range(nh):
            m_old = m_scr[h]
            m_new = jnp.maximum(m_old, jnp.max(scs[h], axis=-1, keepdims=True))
            pt = jnp.exp(scs[h] - m_new)
            acc_scr[h] = jnp.exp(m_old - m_new) * acc_scr[h] + _dot(pt.astype(BF16), v)
            m_scr[h] = m_new

    n_full = t0 // tk

    def sel_body(kt, _):
        sel_tile(kt, False)
        return 0

    lax.fori_loop(0, n_full, sel_body, 0)
    sel_tile(n_full, True)
    o_s = [finish(acc_scr[h]) for h in range(nh)]

    n_w = NSA_WINDOW // tq + 1
    tvw = t0 + lax.broadcasted_iota(jnp.int32, (tq, tq), 0)
    w_bias, w_k0 = [], []
    for i in range(n_w):
        kstart = t0 - NSA_WINDOW + i * tq
        kpos = kstart + lax.broadcasted_iota(jnp.int32, (tq, tq), 1)
        inside = (kpos >= jnp.maximum(tvw - (NSA_WINDOW - 1), 0)) & (kpos <= tvw)
        w_bias.append(jnp.where(inside, 0.0, NEG_INF))
        w_k0.append(pl.multiple_of(jnp.maximum(kstart, 0), tq))
    o_w = []
    for h in range(nh):
        scs = [_dot_nt(qr_scr[h], kw_ref[0, pl.ds(w_k0[i], tq), :]) + w_bias[i] for i in range(n_w)]
        m = jnp.max(scs[0], axis=-1, keepdims=True)
        for i in range(1, n_w):
            m = jnp.maximum(m, jnp.max(scs[i], axis=-1, keepdims=True))
        acc = _dot(jnp.exp(scs[0] - m).astype(BF16), vw_ref[0, pl.ds(w_k0[0], tq), :])
        for i in range(1, n_w):
            acc = acc + _dot(jnp.exp(scs[i] - m).astype(BF16), vw_ref[0, pl.ds(w_k0[i], tq), :])
        o_w.append(finish(acc))

    g = _sigmoid(small_ref[0])
    outs = []
    for h in range(nh):
        b0 = SMALL_NSA_G + 3 * h
        outs.append(g[:, b0:b0 + 1] * o_c[h * tq:(h + 1) * tq] + g[:, b0 + 1:b0 + 2] * o_s[h]
                    + g[:, b0 + 2:b0 + 3] * o_w[h])
    out_ref[0] = jnp.concatenate(outs, axis=1)


def _nsa_consts(s, tk):
    ncp = s // NSA_CMP_STRIDE
    ns = s // NSA_SLC_LEN
    n_cmp = (s - NSA_CMP_LEN) // NSA_CMP_STRIDE + 1
    cmp_start = np.arange(ncp) * NSA_CMP_STRIDE
    slc_start = np.arange(ns) * NSA_SLC_LEN
    overlap = ((cmp_start[:, None] < slc_start[None, :] + NSA_SLC_LEN)
               & (cmp_start[:, None] + NSA_CMP_LEN > slc_start[None, :])).astype(np.float32)
    overlap[n_cmp:] = 0.0
    kidx = np.arange(s)
    e = (kidx[None, :] // NSA_SLC_LEN == np.arange(ns)[:, None]).astype(np.float32)
    e3 = e.reshape(ns, s // tk, tk).transpose(1, 0, 2)
    return jnp.asarray(overlap.T, BF16), jnp.asarray(e3, BF16)


def _nsa_attn(q, small, ct, s1, s2, q_norm, kc, vc, ks, vs, kw, vw, tq, tk):
    b, s, w = q.shape
    hd = HEAD_DIM
    ncp = kc.shape[1]
    ns = s // NSA_SLC_LEN
    ot, e3 = _nsa_consts(s, tk)
    nh = NSA_HEADS
    blk = lambda width: pl.BlockSpec((1, tq, width), lambda bi, ci: (bi, ci, 0))
    per_b = lambda n, width=hd: pl.BlockSpec((1, n, width), lambda bi, ci: (bi, 0, 0))
    return pl.pallas_call(
        _nsa_attn_kernel,
        grid=(b, s // tq),
        in_specs=[blk(w), blk(SMALL_W), blk(hd), blk(hd), blk(hd),
                  pl.BlockSpec((1, hd), lambda bi, ci: (0, 0)),
                  per_b(ncp), per_b(ncp), per_b(s), per_b(s, 2 * hd), per_b(s), per_b(s, 2 * hd),
                  pl.BlockSpec((ns, ncp), lambda bi, ci: (0, 0)),
                  pl.BlockSpec((s // tk, ns, tk), lambda bi, ci: (0, 0, 0))],
        out_specs=blk(w),
        out_shape=jax.ShapeDtypeStruct((b, s, w), F32),
        scratch_shapes=[pltpu.VMEM((ns, tq), F32), pltpu.VMEM((nh, tq, hd), BF16), pltpu.VMEM((nh, tq, 1), F32),
                        pltpu.VMEM((nh, tq, 2 * hd), F32)],
        compiler_params=_cparams(("parallel", "arbitrary")),
        name="nsa_attn",
    )(q, small, ct, s1, s2, q_norm, kc, vc, ks, vs, kw, vw, ot, e3)


def _gdn_kernel(x_ref, prev_ref, small_ref, z_ref, cw_ref, nega_ref, dtb_ref, on_ref, tril_ref, out_ref, st_ref):
    nh, hd, ck = GDN_HEADS, HEAD_DIM, GDN_CHUNK
    r = x_ref.shape[1]
    width = x_ref.shape[2]
    i_blk = pl.program_id(1)

    @pl.when(i_blk == 0)
    def _():
        st_ref[...] = jnp.zeros(st_ref.shape, F32)

    x = x_ref[0]
    prev = jnp.where(i_blk > 0, prev_ref[0], 0.0)
    r8 = lax.broadcasted_iota(jnp.int32, (SUBLANES, width), 0)
    conv = x * cw_ref[GDN_CONV - 1:GDN_CONV, :]
    for back in range(1, GDN_CONV):
        xs = pltpu.roll(x, back, 0)
        head = jnp.where(r8 < back, pltpu.roll(prev, back, 0), xs[0:SUBLANES])
        xs = jnp.concatenate([head, xs[SUBLANES:]], axis=0)
        conv = conv + xs * cw_ref[GDN_CONV - 1 - back:GDN_CONV - back, :]
    act = conv * _sigmoid(conv)

    sm = small_ref[0]
    beta_all = _sigmoid(sm)
    g_all = nega_ref[...] * _softplus(sm + dtb_ref[...])
    zf = z_ref[0]
    tril = tril_ref[...]
    ri = lax.broadcasted_iota(jnp.int32, (ck, ck), 0)
    ci = lax.broadcasted_iota(jnp.int32, (ck, ck), 1)
    lower = ri >= ci
    strict = ri > ci

    eye = (ri == ci).astype(F32)
    right = lax.broadcasted_iota(jnp.int32, (ck, 2 * ck), 1) >= ck
    nchunk = r // ck
    inst = [(cidx, h) for cidx in range(nchunk) for h in range(nh)]

    gcums = []
    for cidx in range(nchunk):
        g3 = _split3(g_all[cidx * ck:(cidx + 1) * ck])
        gcum = _dot(tril, g3[0]) + (_dot(tril, g3[1]) + _dot(tril, g3[2]))
        gcums.append((gcum, gcum.T))
    pre = []
    for cidx, h in inst:
        rs = slice(cidx * ck, (cidx + 1) * ck)
        gcum, gcum_t = gcums[cidx]
        q = act[rs, h * hd:(h + 1) * hd]
        k = act[rs, GDN_W + h * hd:GDN_W + (h + 1) * hd]
        v = act[rs, 2 * GDN_W + h * hd:2 * GDN_W + (h + 1) * hd]
        q = q * lax.rsqrt(jnp.sum(q * q, axis=-1, keepdims=True) + NORM_EPS) * (HEAD_DIM ** -0.5)
        k = k * lax.rsqrt(jnp.sum(k * k, axis=-1, keepdims=True) + NORM_EPS)
        beta = beta_all[rs, SMALL_GDN_B + h:SMALL_GDN_B + h + 1]
        gc = gcum[:, SMALL_GDN_A + h:SMALL_GDN_A + h + 1]
        gr = gcum_t[SMALL_GDN_A + h:SMALL_GDN_A + h + 1, :]
        g_last = gcum[ck - 1:ck, SMALL_GDN_A + h:SMALL_GDN_A + h + 1]
        decay = jnp.where(lower, jnp.exp(jnp.where(lower, gc - gr, 0.0)), 0.0)
        kb = k * beta
        e_gc = jnp.exp(gc)
        rhs = jnp.concatenate([v * beta, kb * e_gc], axis=1).astype(BF16)
        kq = jnp.concatenate([kb, q], axis=0).astype(BF16)
        pre.append(dict(kq=kq, k16=k.astype(BF16), decay=decay, rhs=rhs, q_dec=(q * e_gc).astype(BF16),
                        k_dec_t=(k * jnp.exp(g_last - gc)).T.astype(BF16), e_last=jnp.exp(g_last)))
    for p in pre:
        raw = _dot_nt(p['kq'], p['k16'])
        l_mat = jnp.where(strict, raw[:ck] * p['decay'], 0.0)
        p['a_intra'] = jnp.where(lower, raw[ck:] * p['decay'], 0.0).astype(BF16)
        p['mp'] = jnp.concatenate([l_mat, eye], axis=1)
    for rnd in range(int(math.ceil(math.log2(ck)))):
        for p in pre:
            mp = p['mp']
            res = _dot(mp[:, :ck].astype(BF16), mp.astype(BF16))
            if rnd == 0:
                res = jnp.where(right, -res, res)
            p['mp'] = res + jnp.where(right, mp, 0.0)
    for p in pre:
        uw = _dot(p['mp'][:, ck:].astype(BF16), p['rhs'])
        p['u'] = uw[:, :hd]
        p['wq'] = jnp.concatenate([uw[:, hd:].astype(BF16), p['q_dec']], axis=0)

    out_chunks = []
    for cidx in range(nchunk):
        rs = slice(cidx * ck, (cidx + 1) * ck)
        heads_out = []
        for h in range(nh):
            p = pre[cidx * nh + h]
            state = st_ref[h]
            ws_qs = _dot(p['wq'], state.astype(BF16))
            v_new = (p['u'] - ws_qs[:ck]).astype(BF16)
            ak = jnp.concatenate([p['a_intra'], p['k_dec_t']], axis=0)
            av_kv = _dot(ak, v_new)
            o = ws_qs[ck:] + av_kv[:ck]
            st_ref[h] = state * p['e_last'] + av_kv[ck:]
            zh = zf[rs, h * hd:(h + 1) * hd]
            heads_out.append(_rms(o, on_ref[...]) * (zh * _sigmoid(zh)))
        out_chunks.append(jnp.concatenate(heads_out, axis=1))
    out_ref[0] = jnp.concatenate(out_chunks, axis=0)


def _gdn(qkv, small, z, conv_w, neg_a_row, dtb_row, o_norm, r):
    b, s, width = qkv.shape
    ck = GDN_CHUNK
    tril = jnp.asarray(np.tril(np.ones((ck, ck), np.float32)), BF16)
    nprev = r // SUBLANES
    return pl.pallas_call(
        _gdn_kernel,
        grid=(b, s // r),
        in_specs=[pl.BlockSpec((1, r, width), lambda bi, i: (bi, i, 0)),
                  pl.BlockSpec((1, SUBLANES, width), lambda bi, i: (bi, jnp.maximum(i * nprev - 1, 0), 0)),
                  pl.BlockSpec((1, r, SMALL_W), lambda bi, i: (bi, i, 0)),
                  pl.BlockSpec((1, r, GDN_W), lambda bi, i: (bi, i, 0)),
                  pl.BlockSpec((GDN_CONV, width), lambda bi, i: (0, 0)),
                  pl.BlockSpec((1, SMALL_W), lambda bi, i: (0, 0)),
                  pl.BlockSpec((1, SMALL_W), lambda bi, i: (0, 0)),
                  pl.BlockSpec((1, HEAD_DIM), lambda bi, i: (0, 0)),
                  pl.BlockSpec((ck, ck), lambda bi, i: (0, 0))],
        out_specs=pl.BlockSpec((1, r, GDN_W), lambda bi, i: (bi, i, 0)),
        out_shape=jax.ShapeDtypeStruct((b, s, GDN_W), F32),
        scratch_shapes=[pltpu.VMEM((GDN_HEADS, HEAD_DIM, HEAD_DIM), F32)],
        compiler_params=_cparams(("parallel", "arbitrary")),
        name="gdn",
    )(qkv, qkv, small, z, conv_w, neg_a_row, dtb_row, o_norm, tril)


def _rope_tables(positions):
    half = ROT_DIM // 2
    inv_freq = ROPE_THETA ** (-jnp.arange(half, dtype=F32) / half)
    ang = positions.astype(F32)[..., None] * inv_freq
    cos, sin = jnp.cos(ang), jnp.sin(ang)
    rest = HEAD_DIM - ROT_DIM
    ones = jnp.ones(cos.shape[:-1] + (rest,), F32)
    zeros = jnp.zeros(cos.shape[:-1] + (rest,), F32)
    zh = jnp.zeros_like(sin)
    ct = jnp.concatenate([cos, cos, ones], axis=-1)
    s1 = jnp.concatenate([-sin, zh, zeros], axis=-1)
    s2 = jnp.concatenate([zh, sin, zeros], axis=-1)
    return ct, s1, s2


def _row(vec, offset):
    return jnp.zeros((1, SMALL_W), F32).at[0, offset:offset + vec.shape[0]].set(vec.astype(F32))


def _layer(x2, b, s, ct, s1, s2, attn_norm, w_in, nsa_q_norm, nsa_k_norm, nsa_cmp_pos, ck_w1, ck_w2, cv_w1,
           cv_w2, lam_re, lam_im, log_dt, b_re, b_im, c_re, c_im, s5_d, s5_glu_w, conv_w, a_log, dt_bias,
           o_norm, w_branch, w_out, ffn_norm, w_gate, w_up, w_down, tiles):
    hd = HEAD_DIM
    w_a, w_g = _arrange_w_in(w_in)
    nsa_q, nsa_kv, s5_u, gdn_qkv, gdn_z, sb_qkv, small = _inproj(x2, attn_norm.reshape(1, -1), w_a, tiles['tm'])
    r3 = lambda a: a.reshape(b, s, a.shape[-1])

    kv3 = r3(nsa_kv)
    grp = NSA_CMP_STRIDE * hd
    gk = kv3[:, :, 0:hd].reshape(b, s // NSA_CMP_STRIDE, grp)
    gv = kv3[:, :, hd:2 * hd].reshape(b, s // NSA_CMP_STRIDE, grp)
    pos4 = nsa_cmp_pos.reshape(4, grp)
    kc, vc, ks, vs, kw, vw = _nsa_prep(kv3, gk, gv, pos4, ck_w1.astype(BF16), ck_w2.astype(BF16),
                                       cv_w1.astype(BF16), cv_w2.astype(BF16), nsa_k_norm, ct, s1, s2)
    o_nsa = _nsa_attn(r3(nsa_q), r3(small), ct, s1, s2, nsa_q_norm.reshape(1, hd), kc, vc, ks, vs, kw, vw,
                      tiles['nsa_tq'], tiles['nsa_tk'])

    wb, are, aim, cm = _s5_params(lam_re, lam_im, log_dt, b_re, b_im, c_re, c_im)
    u_tm = r3(s5_u).transpose(1, 0, 2).reshape(s * b, S5_W)
    o_s5 = _s5(u_tm, b, wb, are, aim, cm, s5_d.reshape(1, S5_W), s5_glu_w.astype(BF16), tiles['s5_steps'])
    o_s5 = o_s5.reshape(s, b, S5_W).transpose(1, 0, 2).reshape(b * s, S5_W)

    o_gdn = _gdn(r3(gdn_qkv), r3(small), r3(gdn_z), conv_w, _row(-jnp.exp(a_log), SMALL_GDN_A),
                 _row(dt_bias, SMALL_GDN_A), o_norm.reshape(1, hd), tiles['gdn_rows'])

    o_sb = _sb(r3(sb_qkv), tiles['sb_tq']).reshape(b * s, SB_W)

    x2 = _merge(x2, attn_norm.reshape(1, -1), w_g, (o_nsa.reshape(b * s, NSA_W), o_s5, o_gdn.reshape(b * s, GDN_W),
                                                    o_sb), w_branch.astype(BF16), w_out.astype(BF16), tiles['tm_merge'])
    return _ffn(x2, ffn_norm.reshape(1, -1), w_gate.astype(BF16), w_up.astype(BF16), w_down.astype(BF16),
                tiles['tm_ffn'], tiles['tf'])


def _tiles(b, s):
    t = b * s
    return dict(tm=min(512, t), tm_merge=min(256, t), tm_ffn=min(512, t), tf=D_FF // 2,
                nsa_tq=min(256, s), nsa_tk=min(512, s), s5_steps=min(64, s), gdn_rows=min(256, s), sb_tq=min(256, s))


def kernel(x, positions, attn_norm, w_in, nsa_q_norm, nsa_k_norm, nsa_cmp_pos, nsa_cmp_k_w1, nsa_cmp_k_w2, nsa_cmp_v_w1, nsa_cmp_v_w2, s5_lam_re, s5_lam_im, s5_log_dt, s5_b_re, s5_b_im, s5_c_re, s5_c_im, s5_d, s5_glu_w, gdn_conv_w, gdn_a_log, gdn_dt_bias, gdn_o_norm, w_branch, w_out, ffn_norm, w_gate, w_up, w_down):
    b, s, d = x.shape
    ct, s1, s2 = _rope_tables(positions)
    tiles = _tiles(b, s)
    x2 = x.reshape(b * s, d)
    per_layer = (attn_norm, w_in, nsa_q_norm, nsa_k_norm, nsa_cmp_pos, nsa_cmp_k_w1, nsa_cmp_k_w2, nsa_cmp_v_w1,
                 nsa_cmp_v_w2, s5_lam_re, s5_lam_im, s5_log_dt, s5_b_re, s5_b_im, s5_c_re, s5_c_im, s5_d, s5_glu_w,
                 gdn_conv_w, gdn_a_log, gdn_dt_bias, gdn_o_norm, w_branch, w_out, ffn_norm, w_gate, w_up, w_down)
    for l in range(attn_norm.shape[0]):
        x2 = _layer(x2, b, s, ct, s1, s2, *[p[l] for p in per_layer], tiles)
    return x2.reshape(b, s, d)
```

```python
import functools
import math

import numpy as np
import jax
import jax.numpy as jnp
from jax import lax
from jax.experimental import pallas as pl
from jax.experimental.pallas import tpu as pltpu

F32 = jnp.float32
BF16 = jnp.bfloat16

D_MODEL = 1024
HEAD_DIM = 64
ROT_DIM = HEAD_DIM // 4
ROPE_THETA = 500000.0
NORM_EPS = 1e-6
NEG_INF = -1e30
FORCED = 1e9

NSA_HEADS = 4
NSA_CMP_LEN = 32
NSA_CMP_STRIDE = 16
NSA_SLC_LEN = 64
NSA_TOPK = 16
NSA_WINDOW = 512
NSA_W = NSA_HEADS * HEAD_DIM

S5_GROUPS = 16
S5_GROUP_CH = 16
S5_W = S5_GROUPS * S5_GROUP_CH
S5_STATE = 64
S5_LANES = S5_GROUPS * S5_STATE

GDN_HEADS = 4
GDN_CONV = 4
GDN_CHUNK = 64
GDN_W = GDN_HEADS * HEAD_DIM

SB_HEADS = 4
SB_W = SB_HEADS * HEAD_DIM

N_BRANCH = 4
D_FF = 256 * math.ceil(8 * D_MODEL / (3 * 256))

LANES = 128
SUBLANES = 8

SMALL_NSA_G = 0
SMALL_GDN_A = 12
SMALL_GDN_B = 16
SMALL_W = LANES

IN_SPLITS = (NSA_W, 6 * HEAD_DIM, 3 * NSA_HEADS, S5_W, 3 * GDN_W, GDN_HEADS, GDN_HEADS, GDN_W, 3 * SB_W,
             N_BRANCH * D_MODEL)
IN_OFFSETS = tuple(int(v) for v in np.cumsum((0,) + IN_SPLITS))

LOG2E = 1.4426950408889634
LN2 = 0.6931471805599453

VMEM_LIMIT = 56 * 1024 * 1024


def _cparams(sem):
    return pltpu.CompilerParams(dimension_semantics=sem, vmem_limit_bytes=VMEM_LIMIT)


def _rms(x, gain):
    return x * lax.rsqrt(jnp.mean(x * x, axis=-1, keepdims=True) + NORM_EPS) * gain


def _gelu_tanh(x):
    return 0.5 * x * (1.0 + jnp.tanh(math.sqrt(2.0 / math.pi) * (x + 0.044715 * (x * x * x))))


def _sigmoid(x):
    return 1.0 / (1.0 + jnp.exp(-x))


def _softplus(x):
    return jnp.maximum(x, 0.0) + jnp.log(1.0 + jnp.exp(-jnp.abs(x)))


def _dot(a, b):
    return jnp.dot(a, b, preferred_element_type=F32)


def _dot_nt(a, b):
    return lax.dot_general(a, b, (((1,), (1,)), ((), ())), preferred_element_type=F32)


def _split2(x):
    hi = x.astype(BF16)
    lo = (x - hi.astype(F32)).astype(BF16)
    return hi, lo


def _split3(x):
    hi = x.astype(BF16)
    r = x - hi.astype(F32)
    mid = r.astype(BF16)
    lo = (r - mid.astype(F32)).astype(BF16)
    return hi, mid, lo


def _rope64(xn, ct, s1, s2):
    half = ROT_DIM // 2
    left = jnp.concatenate([xn[:, half:], xn[:, :half]], axis=1)
    right = jnp.concatenate([xn[:, HEAD_DIM - half:], xn[:, :HEAD_DIM - half]], axis=1)
    return xn * ct + left * s1 + right * s2


INPROJ_WIDTHS = (NSA_W, 6 * HEAD_DIM, S5_W, 3 * GDN_W, GDN_W, 3 * SB_W, SMALL_W)
INPROJ_DTYPES = (F32, F32, F32, F32, F32, BF16, F32)


def _inproj_kernel(x_ref, g_ref, w_ref, *out_refs):
    h = _rms(x_ref[...], g_ref[...]).astype(BF16)
    off = 0
    for o_ref, width in zip(out_refs, INPROJ_WIDTHS):
        o_ref[...] = _dot(h, w_ref[:, off:off + width]).astype(o_ref.dtype)
        off += width


INPROJ_S5 = 2


def _inproj(x2, gain, w_a, b, s, tm):
    t, d = x2.shape
    n = w_a.shape[1]
    nsb = s // tm
    out_specs = [pl.BlockSpec((tm, w), lambda i: (i, 0)) for w in INPROJ_WIDTHS]
    out_shape = [jax.ShapeDtypeStruct((t, w), dt) for w, dt in zip(INPROJ_WIDTHS, INPROJ_DTYPES)]
    out_specs[INPROJ_S5] = pl.BlockSpec((tm, S5_W), lambda i: (i % nsb, i // nsb))
    out_shape[INPROJ_S5] = jax.ShapeDtypeStruct((s, b * S5_W), F32)
    return pl.pallas_call(
        _inproj_kernel,
        grid=(t // tm,),
        in_specs=[pl.BlockSpec((tm, d), lambda i: (i, 0)),
                  pl.BlockSpec((1, d), lambda i: (0, 0)),
                  pl.BlockSpec((d, n), lambda i: (0, 0))],
        out_specs=out_specs,
        out_shape=out_shape,
        compiler_params=_cparams(("parallel",)),
        name="inproj",
    )(x2, gain, w_a)


_W_GROUPS = ((IN_OFFSETS[0], NSA_W), (IN_OFFSETS[1], 6 * HEAD_DIM), (IN_OFFSETS[3], S5_W), (IN_OFFSETS[4], 3 * GDN_W),
             (IN_OFFSETS[7], GDN_W), (IN_OFFSETS[8], 3 * SB_W))
_W_A = sum(w for _, w in _W_GROUPS) + SMALL_W


def _wprep_kernel(w_ref, small_ref, wa_ref, wg_ref):
    dst = 0
    for src, width in _W_GROUPS:
        wa_ref[0, :, dst:dst + width] = w_ref[0, :, src:src + width].astype(BF16)
        dst += width
    wa_ref[0, :, dst:dst + SMALL_W] = small_ref[0].astype(BF16)
    wg_ref[0] = w_ref[0, :, IN_OFFSETS[9]:IN_OFFSETS[10]].astype(BF16)


def _arrange_w_in(w_in):
    nl, d, n = w_in.shape
    o = IN_OFFSETS
    small = jnp.concatenate([w_in[:, :, o[2]:o[3]], w_in[:, :, o[5]:o[6]], w_in[:, :, o[6]:o[7]]], axis=2)
    small = jnp.pad(small, ((0, 0), (0, 0), (0, SMALL_W - small.shape[2])))
    rb = 256
    return pl.pallas_call(
        _wprep_kernel,
        grid=(nl, d // rb),
        in_specs=[pl.BlockSpec((1, rb, n), lambda l, i: (l, i, 0)),
                  pl.BlockSpec((1, rb, SMALL_W), lambda l, i: (l, i, 0))],
        out_specs=[pl.BlockSpec((1, rb, _W_A), lambda l, i: (l, i, 0)),
                   pl.BlockSpec((1, rb, N_BRANCH * d), lambda l, i: (l, i, 0))],
        out_shape=[jax.ShapeDtypeStruct((nl, d, _W_A), BF16), jax.ShapeDtypeStruct((nl, d, N_BRANCH * d), BF16)],
        compiler_params=_cparams(("parallel", "parallel")),
        name="wprep",
    )(w_in, small)


def _merge_kernel(x_ref, g_ref, wg_ref, o1_ref, o2_ref, o3_ref, o4_ref, p_ref, wo_ref, out_ref):
    x = x_ref[...]
    d = x.shape[1]
    h = _rms(x, g_ref[...]).astype(BF16)
    merged = jnp.zeros(x.shape, F32)
    for m, o_ref in enumerate((o1_ref, o2_ref, o3_ref, o4_ref)):
        gate = _sigmoid(_dot(h, wg_ref[:, m * d:(m + 1) * d]))
        merged = merged + gate * _dot(o_ref[...].astype(BF16), p_ref[m])
    out_ref[...] = x + _dot(merged.astype(BF16), wo_ref[...])


def _merge(x2, gain, w_g, outs, w_branch, w_out, tm):
    t, d = x2.shape
    bw = outs[0].shape[1]
    return pl.pallas_call(
        _merge_kernel,
        grid=(t // tm,),
        in_specs=[pl.BlockSpec((tm, d), lambda i: (i, 0)),
                  pl.BlockSpec((1, d), lambda i: (0, 0)),
                  pl.BlockSpec((d, N_BRANCH * d), lambda i: (0, 0))]
                 + [pl.BlockSpec((tm, bw), lambda i: (i, 0)) for _ in range(N_BRANCH)]
                 + [pl.BlockSpec((N_BRANCH, bw, d), lambda i: (0, 0, 0)),
                    pl.BlockSpec((d, d), lambda i: (0, 0))],
        out_specs=pl.BlockSpec((tm, d), lambda i: (i, 0)),
        out_shape=jax.ShapeDtypeStruct((t, d), F32),
        compiler_params=_cparams(("parallel",)),
        name="merge",
    )(x2, gain, w_g, *outs, w_branch, w_out)


def _ffn_kernel(x_ref, g_ref, wg_ref, wu_ref, wd_ref, out_ref, h_scr, acc_scr):
    j = pl.program_id(1)

    @pl.when(j == 0)
    def _():
        h_scr[...] = _rms(x_ref[...], g_ref[...]).astype(BF16)
        acc_scr[...] = jnp.zeros(acc_scr.shape, F32)

    h = h_scr[...]
    a = _dot(h, wg_ref[...])
    u = _dot(h, wu_ref[...])
    act = (a * _sigmoid(a) * u).astype(BF16)
    acc_scr[...] += _dot(act, wd_ref[...])

    @pl.when(j == pl.num_programs(1) - 1)
    def _():
        out_ref[...] = x_ref[...] + acc_scr[...]


def _ffn(x2, gain, w_gate, w_up, w_down, tm, tf):
    t, d = x2.shape
    f = w_gate.shape[1]
    return pl.pallas_call(
        _ffn_kernel,
        grid=(t // tm, f // tf),
        in_specs=[pl.BlockSpec((tm, d), lambda i, j: (i, 0)),
                  pl.BlockSpec((1, d), lambda i, j: (0, 0)),
                  pl.BlockSpec((d, tf), lambda i, j: (0, j)),
                  pl.BlockSpec((d, tf), lambda i, j: (0, j)),
                  pl.BlockSpec((tf, d), lambda i, j: (j, 0))],
        out_specs=pl.BlockSpec((tm, d), lambda i, j: (i, 0)),
        out_shape=jax.ShapeDtypeStruct((t, d), F32),
        scratch_shapes=[pltpu.VMEM((tm, d), BF16), pltpu.VMEM((tm, d), F32)],
        compiler_params=_cparams(("parallel", "arbitrary")),
        name="ffn",
    )(x2, gain, w_gate, w_up, w_down)


def _s5_kernel(u_ref, wb_ref, are_ref, aim_ref, cm_ref, d_ref, glu_ref, out_ref, st_ref, bu_ref, y_scr):
    nb = st_ref.shape[0]
    p = are_ref.shape[1]
    steps = u_ref.shape[0] // nb

    @pl.when(pl.program_id(0) == 0)
    def _():
        st_ref[...] = jnp.zeros(st_ref.shape, F32)

    u = u_ref[...]
    bu_ref[...] = _dot(u.astype(BF16), wb_ref[...])
    are = jnp.broadcast_to(are_ref[...], (nb, p))
    aim = jnp.broadcast_to(aim_ref[...], (nb, p))

    def body(t, carry):
        xr, xi = carry
        r0 = pl.multiple_of(t * nb, nb)
        br = bu_ref[pl.ds(r0, nb), 0:p]
        bi = bu_ref[pl.ds(r0, nb), p:2 * p]
        nr = are * xr - aim * xi + br
        ni = are * xi + aim * xr + bi
        bu_ref[pl.ds(r0, nb), 0:p] = nr
        bu_ref[pl.ds(r0, nb), p:2 * p] = ni
        return nr, ni

    xr, xi = lax.fori_loop(0, steps, body, (st_ref[:, 0:p], st_ref[:, p:2 * p]))
    st_ref[:, 0:p] = xr
    st_ref[:, p:2 * p] = xi

    y = _dot(bu_ref[...].astype(BF16), cm_ref[...]) + d_ref[...] * u
    y = _gelu_tanh(y)
    z = _dot(y.astype(BF16), glu_ref[...])
    w = z.shape[1] // 2
    y = z[:, :w] * _sigmoid(z[:, w:])
    for j in range(w // LANES):
        y_scr[j] = y[:, j * LANES:(j + 1) * LANES]
    for bi in range(nb):
        for j in range(w // LANES):
            out_ref[bi, :, j * LANES:(j + 1) * LANES] = y_scr[j, pl.ds(bi, steps, stride=nb), :]


def _s5_params(lam_re, lam_im, log_dt, b_re, b_im, c_re, c_im):
    dt = jnp.exp(log_dt)[:, None]
    mag = jnp.exp(lam_re * dt)
    ab_re, ab_im = mag * jnp.cos(lam_im * dt), mag * jnp.sin(lam_im * dt)
    den = lam_re * lam_re + lam_im * lam_im
    f_re = ((ab_re - 1.0) * lam_re + ab_im * lam_im) / den
    f_im = (ab_im * lam_re - (ab_re - 1.0) * lam_im) / den
    bb_re = f_re[..., None] * b_re - f_im[..., None] * b_im
    bb_im = f_re[..., None] * b_im + f_im[..., None] * b_re
    eye = jnp.eye(S5_GROUPS, dtype=F32)
    wb_re = jnp.einsum('gpc,gh->gchp', bb_re, eye).reshape(S5_W, S5_LANES)
    wb_im = jnp.einsum('gpc,gh->gchp', bb_im, eye).reshape(S5_W, S5_LANES)
    wb = jnp.concatenate([wb_re, wb_im], axis=1).astype(BF16)
    cm_re = jnp.einsum('gcp,gh->hpgc', c_re, eye).reshape(S5_LANES, S5_W)
    cm_im = jnp.einsum('gcp,gh->hpgc', c_im, eye).reshape(S5_LANES, S5_W)
    cm = jnp.concatenate([cm_re, -cm_im], axis=0).astype(BF16)
    return wb, ab_re.reshape(1, S5_LANES), ab_im.reshape(1, S5_LANES), cm


def _s5(u_tm, nb, wb, are, aim, cm, d_skip, glu_w, steps):
    rows, w = u_tm.shape
    r = steps * nb
    p = are.shape[1]
    s = rows // nb
    return pl.pallas_call(
        _s5_kernel,
        grid=(rows // r,),
        in_specs=[pl.BlockSpec((r, w), lambda i: (i, 0)),
                  pl.BlockSpec((w, 2 * p), lambda i: (0, 0)),
                  pl.BlockSpec((1, p), lambda i: (0, 0)),
                  pl.BlockSpec((1, p), lambda i: (0, 0)),
                  pl.BlockSpec((2 * p, w), lambda i: (0, 0)),
                  pl.BlockSpec((1, w), lambda i: (0, 0)),
                  pl.BlockSpec((w, 2 * w), lambda i: (0, 0))],
        out_specs=pl.BlockSpec((nb, steps, w), lambda i: (0, i, 0)),
        out_shape=jax.ShapeDtypeStruct((nb, s, w), F32),
        scratch_shapes=[pltpu.VMEM((nb, 2 * p), F32), pltpu.VMEM((r, 2 * p), F32),
                        pltpu.VMEM((w // LANES, r, LANES), F32)],
        compiler_params=_cparams(("arbitrary",)),
        name="s5",
    )(u_tm, wb, are, aim, cm, d_skip, glu_w)


def _sb_kernel(q_ref, k_ref, v_ref, u_ref, out_ref, q_scr, k_scr, v_scr, acc_scr, carry_scr):
    nh, hd = SB_HEADS, HEAD_DIM
    c = pl.program_id(1)
    tq = q_ref.shape[1]

    @pl.when(c == 0)
    def _():
        for h in range(nh):
            k_scr[h] = k_ref[0, :, h * hd:(h + 1) * hd]
            v_scr[h] = v_ref[0, :, h * hd:(h + 1) * hd]

    q = q_ref[0]
    for h in range(nh):
        q_scr[h] = q[:, h * hd:(h + 1) * hd] * jnp.asarray(hd ** -0.5, BF16)
    acc_scr[...] = jnp.zeros(acc_scr.shape, F32)
    carry_scr[...] = jnp.zeros(carry_scr.shape, F32)

    def chunk(k0, diagonal):
        if diagonal:
            strict = (lax.broadcasted_iota(jnp.int32, (tq, tq), 1) < lax.broadcasted_iota(jnp.int32, (tq, tq), 0))
        zs, sps, parts = [], [], []
        for h in range(nh):
            z = _dot_nt(q_scr[h], k_scr[h, pl.ds(k0, tq), :])
            sp = jnp.maximum(z, 0.0) + LN2 * jnp.log2(1.0 + jnp.exp2(jnp.abs(z) * (-LOG2E)))
            if diagonal:
                sp = jnp.where(strict, sp, 0.0)
            zs.append(z)
            sps.append(sp)
            parts.append(_split2(sp))
        befores = [_dot(hi, u_ref[...]) + _dot(lo, u_ref[...]) for hi, lo in parts]
        for h in range(nh):
            w = jnp.exp((zs[h] - sps[h]) - befores[h] + carry_scr[h])
            if diagonal:
                w = jnp.where(strict, w, 0.0)
            acc_scr[h] += _dot(w.astype(BF16), v_scr[h, pl.ds(k0, tq), :])
            carry_scr[h] -= befores[h][:, 0:1] + sps[h][:, 0:1]

    chunk(pl.multiple_of(c * tq, tq), True)

    def body(i, _):
        chunk(pl.multiple_of((c - i) * tq, tq), False)
        return 0

    lax.fori_loop(1, c + 1, body, 0)
    out_ref[0] = jnp.concatenate([acc_scr[h] for h in range(nh)], axis=1)


def _sb_consts(tq):
    i = np.arange(tq)
    return jnp.asarray((i[:, None] > i[None, :]).astype(np.float32), BF16)


def _sb(qkv, tq):
    b, s, w3 = qkv.shape
    w = w3 // 3
    nh, hd = SB_HEADS, HEAD_DIM
    return pl.pallas_call(
        _sb_kernel,
        grid=(b, s // tq),
        in_specs=[pl.BlockSpec((1, tq, w), lambda bi, ci: (bi, ci, 0)),
                  pl.BlockSpec((1, s, w), lambda bi, ci: (bi, 0, 1)),
                  pl.BlockSpec((1, s, w), lambda bi, ci: (bi, 0, 2)),
                  pl.BlockSpec((tq, tq), lambda bi, ci: (0, 0))],
        out_specs=pl.BlockSpec((1, tq, w), lambda bi, ci: (bi, ci, 0)),
        out_shape=jax.ShapeDtypeStruct((b, s, w), F32),
        scratch_shapes=[pltpu.VMEM((nh, tq, hd), BF16), pltpu.VMEM((nh, s, hd), BF16),
                        pltpu.VMEM((nh, s, hd), BF16), pltpu.VMEM((nh, tq, hd), F32),
                        pltpu.VMEM((nh, tq, 1), F32)],
        compiler_params=_cparams(("parallel", "arbitrary")),
        name="sb",
    )(qkv, qkv, qkv, _sb_consts(tq))


def _nsa_prep_kernel(kv_ref, gk_ref, gv_ref, pos_ref, w1k_ref, w2k_ref, w1v_ref, w2v_ref, kn_ref,
                     ct_ref, s1_ref, s2_ref, kc_ref, vc_ref, ks_ref, vs_ref, kw_ref, vw_ref):
    hd = HEAD_DIM
    ncp = gk_ref.shape[1]
    half = gk_ref.shape[2]
    rowi = lax.broadcasted_iota(jnp.int32, (ncp, w1k_ref.shape[1]), 0)

    def compress(g_ref, pos_t, pos_b, w1_ref, w2_ref):
        g = g_ref[0]
        top = _dot((g + pos_t).astype(BF16), w1_ref[0:half, :])
        bot = _dot((g + pos_b).astype(BF16), w1_ref[half:2 * half, :])
        bot = jnp.where(rowi < ncp - 1, pltpu.roll(bot, ncp - 1, 0), 0.0)
        return _dot(_gelu_tanh(top + bot).astype(BF16), w2_ref[...])

    kc = compress(gk_ref, pos_ref[0:1, :], pos_ref[1:2, :], w1k_ref, w2k_ref)
    kc_ref[0] = _rms(kc, kn_ref[0:1, :]).astype(BF16)
    vc_ref[0] = compress(gv_ref, pos_ref[2:3, :], pos_ref[3:4, :], w1v_ref, w2v_ref).astype(BF16)

    ct, s1, s2 = ct_ref[0], s1_ref[0], s2_ref[0]
    ones_col = (lax.broadcasted_iota(jnp.int32, (kv_ref.shape[1], hd), 1) == 0).astype(BF16)
    k_s = kv_ref[0, :, 2 * hd:3 * hd]
    ks_ref[0] = _rope64(_rms(k_s, kn_ref[1:2, :]), ct, s1, s2).astype(BF16)
    vs_ref[0] = jnp.concatenate([kv_ref[0, :, 3 * hd:4 * hd].astype(BF16), ones_col], axis=1)
    k_w = kv_ref[0, :, 4 * hd:5 * hd]
    kw_ref[0] = _rope64(_rms(k_w, kn_ref[2:3, :]), ct, s1, s2).astype(BF16)
    vw_ref[0] = jnp.concatenate([kv_ref[0, :, 5 * hd:6 * hd].astype(BF16), ones_col], axis=1)


def _nsa_prep(kv, gk, gv, pos4, w1k, w2k, w1v, w2v, k_norm, ct, s1, s2):
    b, s, _ = kv.shape
    ncp, half = gk.shape[1], gk.shape[2]
    hid = w1k.shape[1]
    hd = HEAD_DIM
    full3 = lambda shp: pl.BlockSpec((1,) + shp, lambda i: (i, 0, 0))
    const2 = lambda shp: pl.BlockSpec(shp, lambda i: (0, 0))
    return pl.pallas_call(
        _nsa_prep_kernel,
        grid=(b,),
        in_specs=[full3((s, 6 * hd)), full3((ncp, half)), full3((ncp, half)), const2((4, half)),
                  const2((2 * half, hid)), const2((hid, hd)), const2((2 * half, hid)), const2((hid, hd)),
                  const2((3, hd)), full3((s, hd)), full3((s, hd)), full3((s, hd))],
        out_specs=[full3((ncp, hd)), full3((ncp, hd)), full3((s, hd)), full3((s, 2 * hd)), full3((s, hd)),
                   full3((s, 2 * hd))],
        out_shape=[jax.ShapeDtypeStruct((b, ncp, hd), BF16), jax.ShapeDtypeStruct((b, ncp, hd), BF16)]
                  + [jax.ShapeDtypeStruct((b, s, hd), BF16), jax.ShapeDtypeStruct((b, s, 2 * hd), BF16)] * 2,
        compiler_params=_cparams(("parallel",)),
        name="nsa_prep",
    )(kv, gk, gv, pos4, w1k, w2k, w1v, w2v, k_norm, ct, s1, s2)


def _nsa_attn_kernel(q_ref, small_ref, ct_ref, s1_ref, s2_ref, qn_ref, kc_ref, vc_ref, ks_ref, vs_ref,
                     kw_ref, vw_ref, ot_ref, e3_ref, out_ref, imp_scr, qr_scr, m_scr, acc_scr):
    nh, hd = NSA_HEADS, HEAD_DIM
    tq = q_ref.shape[1]
    rows = nh * tq
    ncp = kc_ref.shape[1]
    ns = ot_ref.shape[0]
    tk = e3_ref.shape[2]
    top_n = min(NSA_TOPK, ns)
    scale = HEAD_DIM ** -0.5
    c = pl.program_id(1)
    t0 = c * tq

    q = q_ref[0]
    ct, s1, s2 = ct_ref[0], s1_ref[0], s2_ref[0]
    qn_l, qr_l = [], []
    for h in range(nh):
        qn = _rms(q[:, h * hd:(h + 1) * hd], qn_ref[...])
        qn_l.append((qn * scale).astype(BF16))
        qr_l.append((_rope64(qn, ct, s1, s2) * scale).astype(BF16))
    qn_all = jnp.concatenate(qn_l, axis=0)

    def tvec(shape):
        return t0 + (lax.broadcasted_iota(jnp.int32, shape, 0) & (tq - 1))

    s = _dot_nt(qn_all, kc_ref[0])
    cmp_end = lax.broadcasted_iota(jnp.int32, (rows, ncp), 1) * NSA_CMP_STRIDE + (NSA_CMP_LEN - 1)
    mask = cmp_end <= tvec((rows, ncp))
    s = jnp.where(mask, s, NEG_INF)
    p = jnp.where(mask, jnp.exp(s - jnp.max(s, axis=-1, keepdims=True)), 0.0)
    p = p / jnp.maximum(jnp.sum(p, axis=-1, keepdims=True), 1e-30)
    o_c = _dot(p.astype(BF16), vc_ref[0])

    psum = p[0:tq]
    for h in range(1, nh):
        psum = psum + p[h * tq:(h + 1) * tq]
    ph, plo = _split2(psum)
    imp = _dot_nt(ot_ref[...], ph) + _dot_nt(ot_ref[...], plo)
    jb = lax.broadcasted_iota(jnp.int32, (ns, tq), 0)
    tl = t0 + lax.broadcasted_iota(jnp.int32, (ns, tq), 1)
    forced = (jb == (tl // NSA_SLC_LEN)) | (jb == 0)
    imp = jnp.where(forced, FORCED, jnp.where(jb * NSA_SLC_LEN <= tl, imp, NEG_INF))
    imp_scr[...] = imp
    cnt = jnp.zeros((ns, tq), F32)
    for i in range(ns):
        bi = imp_scr[i:i + 1, :]
        ge = jnp.where(bi >= imp, 1.0, 0.0)
        gt = jnp.where(bi > imp, 1.0, 0.0)
        cnt = cnt + jnp.where(jb > i, ge, gt)
    sel = jnp.where(cnt < top_n, 1.0, 0.0).T.astype(BF16)

    def finish(acc):
        return acc[:, :hd] / jnp.maximum(acc[:, hd:hd + 1], 1e-30)

    m_scr[...] = jnp.full(m_scr.shape, NEG_INF, F32)
    acc_scr[...] = jnp.zeros(acc_scr.shape, F32)
    for h in range(nh):
        qr_scr[h] = qr_l[h]

    def sel_tile(kt, causal):
        k0 = pl.multiple_of(kt * tk, tk)
        bias = (_dot(sel, e3_ref[kt]) - 1.0) * (-NEG_INF)
        if causal:
            kpos = k0 + lax.broadcasted_iota(jnp.int32, (tq, tk), 1)
            bias = jnp.where(kpos <= t0 + lax.broadcasted_iota(jnp.int32, (tq, tk), 0), bias, NEG_INF)
        k = ks_ref[0, pl.ds(k0, tk), :]
        v = vs_ref[0, pl.ds(k0, tk), :]
        scs, m_olds, m_news, pts = {}, {}, {}, {}
        for step in range(nh + 2):
            if step < nh:
                scs[step] = _dot_nt(qr_scr[step], k) + bias
            h = step - 1
            if 0 <= h < nh:
                m_olds[h] = m_scr[h]
                m_news[h] = jnp.maximum(m_olds[h], jnp.max(scs[h], axis=-1, keepdims=True))
                pts[h] = jnp.exp(scs[h] - m_news[h]).astype(BF16)
            h = step - 2
            if 0 <= h < nh:
                acc_scr[h] = jnp.exp(m_olds[h] - m_news[h]) * acc_scr[h] + _dot(pts[h], v)
                m_scr[h] = m_news[h]

    n_full = t0 // tk

    def sel_body(kt, _):
        sel_tile(kt, False)
        return 0

    lax.fori_loop(0, n_full, sel_body, 0)
    sel_tile(n_full, True)
    o_s = [finish(acc_scr[h]) for h in range(nh)]

    n_w = NSA_WINDOW // tq + 1
    tvw = t0 + lax.broadcasted_iota(jnp.int32, (tq, tq), 0)
    w_bias, w_k0 = [], []
    for i in range(n_w):
        kstart = t0 - NSA_WINDOW + i * tq
        kpos = kstart + lax.broadcasted_iota(jnp.int32, (tq, tq), 1)
        inside = (kpos >= jnp.maximum(tvw - (NSA_WINDOW - 1), 0)) & (kpos <= tvw)
        w_bias.append(jnp.where(inside, 0.0, NEG_INF))
        w_k0.append(pl.multiple_of(jnp.maximum(kstart, 0), tq))
    w_sc = [[_dot_nt(qr_scr[h], kw_ref[0, pl.ds(w_k0[i], tq), :]) + w_bias[i] for i in range(n_w)]
            for h in range(nh)]
    w_m = []
    for h in range(nh):
        tile_max = w_sc[h][0]
        for i in range(1, n_w):
            tile_max = jnp.maximum(tile_max, w_sc[h][i])
        w_m.append(jnp.max(tile_max, axis=-1, keepdims=True))
    o_w = []
    for h in range(nh):
        acc = _dot(jnp.exp(w_sc[h][0] - w_m[h]).astype(BF16), vw_ref[0, pl.ds(w_k0[0], tq), :])
        for i in range(1, n_w):
            acc = acc + _dot(jnp.exp(w_sc[h][i] - w_m[h]).astype(BF16), vw_ref[0, pl.ds(w_k0[i], tq), :])
        o_w.append(finish(acc))

    g = _sigmoid(small_ref[0])
    outs = []
    for h in range(nh):
        b0 = SMALL_NSA_G + 3 * h
        outs.append(g[:, b0:b0 + 1] * o_c[h * tq:(h + 1) * tq] + g[:, b0 + 1:b0 + 2] * o_s[h]
                    + g[:, b0 + 2:b0 + 3] * o_w[h])
    out_ref[0] = jnp.concatenate(outs, axis=1)


def _nsa_consts(s, tk):
    ncp = s // NSA_CMP_STRIDE
    ns = s // NSA_SLC_LEN
    n_cmp = (s - NSA_CMP_LEN) // NSA_CMP_STRIDE + 1
    cmp_start = np.arange(ncp) * NSA_CMP_STRIDE
    slc_start = np.arange(ns) * NSA_SLC_LEN
    overlap = ((cmp_start[:, None] < slc_start[None, :] + NSA_SLC_LEN)
               & (cmp_start[:, None] + NSA_CMP_LEN > slc_start[None, :])).astype(np.float32)
    overlap[n_cmp:] = 0.0
    kidx = np.arange(s)
    e = (kidx[None, :] // NSA_SLC_LEN == np.arange(ns)[:, None]).astype(np.float32)
    e3 = e.reshape(ns, s // tk, tk).transpose(1, 0, 2)
    return jnp.asarray(overlap.T, BF16), jnp.asarray(e3, BF16)


def _nsa_attn(q, small, ct, s1, s2, q_norm, kc, vc, ks, vs, kw, vw, tq, tk):
    b, s, w = q.shape
    hd = HEAD_DIM
    ncp = kc.shape[1]
    ns = s // NSA_SLC_LEN
    ot, e3 = _nsa_consts(s, tk)
    nh = NSA_HEADS
    blk = lambda width: pl.BlockSpec((1, tq, width), lambda bi, ci: (bi, ci, 0))
    per_b = lambda n, width=hd: pl.BlockSpec((1, n, width), lambda bi, ci: (bi, 0, 0))
    return pl.pallas_call(
        _nsa_attn_kernel,
        grid=(b, s // tq),
        in_specs=[blk(w), blk(SMALL_W), blk(hd), blk(hd), blk(hd),
                  pl.BlockSpec((1, hd), lambda bi, ci: (0, 0)),
                  per_b(ncp), per_b(ncp), per_b(s), per_b(s, 2 * hd), per_b(s), per_b(s, 2 * hd),
                  pl.BlockSpec((ns, ncp), lambda bi, ci: (0, 0)),
                  pl.BlockSpec((s // tk, ns, tk), lambda bi, ci: (0, 0, 0))],
        out_specs=blk(w),
        out_shape=jax.ShapeDtypeStruct((b, s, w), F32),
        scratch_shapes=[pltpu.VMEM((ns, tq), F32), pltpu.VMEM((nh, tq, hd), BF16), pltpu.VMEM((nh, tq, 1), F32),
                        pltpu.VMEM((nh, tq, 2 * hd), F32)],
        compiler_params=_cparams(("parallel", "arbitrary")),
        name="nsa_attn",
    )(q, small, ct, s1, s2, q_norm, kc, vc, ks, vs, kw, vw, ot, e3)


def _gdn_kernel(x_ref, prev_ref, small_ref, z_ref, cw_ref, nega_ref, dtb_ref, on_ref, tril_ref, out_ref, st_ref):
    nh, hd, ck = GDN_HEADS, HEAD_DIM, GDN_CHUNK
    r = x_ref.shape[1]
    width = x_ref.shape[2]
    i_blk = pl.program_id(1)

    @pl.when(i_blk == 0)
    def _():
        st_ref[...] = jnp.zeros(st_ref.shape, F32)

    x = x_ref[0]
    prev = jnp.where(i_blk > 0, prev_ref[0], 0.0)
    r8 = lax.broadcasted_iota(jnp.int32, (SUBLANES, width), 0)
    conv = x * cw_ref[GDN_CONV - 1:GDN_CONV, :]
    for back in range(1, GDN_CONV):
        xs = pltpu.roll(x, back, 0)
        head = jnp.where(r8 < back, pltpu.roll(prev, back, 0), xs[0:SUBLANES])
        xs = jnp.concatenate([head, xs[SUBLANES:]], axis=0)
        conv = conv + xs * cw_ref[GDN_CONV - 1 - back:GDN_CONV - back, :]
    act = conv * _sigmoid(conv)

    sm = small_ref[0]
    beta_all = _sigmoid(sm)
    g_all = nega_ref[...] * _softplus(sm + dtb_ref[...])
    zf = z_ref[0]
    tril = tril_ref[...]
    ri = lax.broadcasted_iota(jnp.int32, (ck, ck), 0)
    ci = lax.broadcasted_iota(jnp.int32, (ck, ck), 1)
    lower = ri >= ci
    strict = ri > ci

    eye = (ri == ci).astype(F32)
    right = lax.broadcasted_iota(jnp.int32, (ck, 2 * ck), 1) >= ck
    nchunk = r // ck
    inst = [(cidx, h) for cidx in range(nchunk) for h in range(nh)]

    gcums = []
    for cidx in range(nchunk):
        g3 = _split3(g_all[cidx * ck:(cidx + 1) * ck])
        gcum = _dot(tril, g3[0]) + (_dot(tril, g3[1]) + _dot(tril, g3[2]))
        gcums.append((gcum, gcum.T))
    pre = []
    for cidx, h in inst:
        rs = slice(cidx * ck, (cidx + 1) * ck)
        gcum, gcum_t = gcums[cidx]
        q = act[rs, h * hd:(h + 1) * hd]
        k = act[rs, GDN_W + h * hd:GDN_W + (h + 1) * hd]
        v = act[rs, 2 * GDN_W + h * hd:2 * GDN_W + (h + 1) * hd]
        q = q * lax.rsqrt(jnp.sum(q * q, axis=-1, keepdims=True) + NORM_EPS) * (HEAD_DIM ** -0.5)
        k = k * lax.rsqrt(jnp.sum(k * k, axis=-1, keepdims=True) + NORM_EPS)
        beta = beta_all[rs, SMALL_GDN_B + h:SMALL_GDN_B + h + 1]
        gc = gcum[:, SMALL_GDN_A + h:SMALL_GDN_A + h + 1]
        gr = gcum_t[SMALL_GDN_A + h:SMALL_GDN_A + h + 1, :]
        g_last = gcum[ck - 1:ck, SMALL_GDN_A + h:SMALL_GDN_A + h + 1]
        decay = jnp.where(lower, jnp.exp(jnp.where(lower, gc - gr, 0.0)), 0.0)
        kb = k * beta
        e_gc = jnp.exp(gc)
        rhs = jnp.concatenate([v * beta, kb * e_gc], axis=1).astype(BF16)
        kq = jnp.concatenate([kb, q], axis=0).astype(BF16)
        pre.append(dict(kq=kq, k16=k.astype(BF16), decay=decay, rhs=rhs, q_dec=q * e_gc,
                        k_dec_t=(k * jnp.exp(g_last - gc)).T.astype(BF16), e_last=jnp.exp(g_last)))
    for p in pre:
        raw = _dot_nt(p['kq'], p['k16'])
        l_mat = jnp.where(strict, raw[:ck] * p['decay'], 0.0)
        p['a_intra'] = jnp.where(lower, raw[ck:] * p['decay'], 0.0).astype(BF16)
        p['mp'] = jnp.concatenate([l_mat, eye], axis=1)
    for rnd in range(int(math.ceil(math.log2(ck)))):
        for p in pre:
            mp = p['mp']
            res = _dot(mp[:, :ck].astype(BF16), mp.astype(BF16))
            if rnd == 0:
                res = jnp.where(right, -res, res)
            p['mp'] = res + jnp.where(right, mp, 0.0)
    for p in pre:
        uw = _dot(p['mp'][:, ck:].astype(BF16), p['rhs'])
        ak = jnp.concatenate([p['a_intra'], p['k_dec_t']], axis=0)
        res = _dot(ak, uw.astype(BF16))
        p['o_const'] = res[:ck, :hd]
        p['s_const'] = res[ck:, :hd]
        p['mq'] = jnp.concatenate([-res[ck:, hd:], p['q_dec'] - res[:ck, hd:]], axis=0).astype(BF16)

    out_chunks = []
    for cidx in range(nchunk):
        rs = slice(cidx * ck, (cidx + 1) * ck)
        heads_out = []
        for h in range(nh):
            p = pre[cidx * nh + h]
            state = st_ref[h]
            r2 = _dot(p['mq'], state.astype(BF16))
            o = r2[hd:] + p['o_const']
            st_ref[h] = state * p['e_last'] + (r2[:hd] + p['s_const'])
            zh = zf[rs, h * hd:(h + 1) * hd]
            heads_out.append(_rms(o, on_ref[...]) * (zh * _sigmoid(zh)))
        out_chunks.append(jnp.concatenate(heads_out, axis=1))
    out_ref[0] = jnp.concatenate(out_chunks, axis=0)


def _gdn(qkv, small, z, conv_w, neg_a_row, dtb_row, o_norm, r):
    b, s, width = qkv.shape
    ck = GDN_CHUNK
    tril = jnp.asarray(np.tril(np.ones((ck, ck), np.float32)), BF16)
    nprev = r // SUBLANES
    return pl.pallas_call(
        _gdn_kernel,
        grid=(b, s // r),
        in_specs=[pl.BlockSpec((1, r, width), lambda bi, i: (bi, i, 0)),
                  pl.BlockSpec((1, SUBLANES, width), lambda bi, i: (bi, jnp.maximum(i * nprev - 1, 0), 0)),
                  pl.BlockSpec((1, r, SMALL_W), lambda bi, i: (bi, i, 0)),
                  pl.BlockSpec((1, r, GDN_W), lambda bi, i: (bi, i, 0)),
                  pl.BlockSpec((GDN_CONV, width), lambda bi, i: (0, 0)),
                  pl.BlockSpec((1, SMALL_W), lambda bi, i: (0, 0)),
                  pl.BlockSpec((1, SMALL_W), lambda bi, i: (0, 0)),
                  pl.BlockSpec((1, HEAD_DIM), lambda bi, i: (0, 0)),
                  pl.BlockSpec((ck, ck), lambda bi, i: (0, 0))],
        out_specs=pl.BlockSpec((1, r, GDN_W), lambda bi, i: (bi, i, 0)),
        out_shape=jax.ShapeDtypeStruct((b, s, GDN_W), F32),
        scratch_shapes=[pltpu.VMEM((GDN_HEADS, HEAD_DIM, HEAD_DIM), F32)],
        compiler_params=_cparams(("parallel", "arbitrary")),
        name="gdn",
    )(qkv, qkv, small, z, conv_w, neg_a_row, dtb_row, o_norm, tril)


def _rope_tables(positions):
    half = ROT_DIM // 2
    inv_freq = ROPE_THETA ** (-jnp.arange(half, dtype=F32) / half)
    ang = positions.astype(F32)[..., None] * inv_freq
    cos, sin = jnp.cos(ang), jnp.sin(ang)
    rest = HEAD_DIM - ROT_DIM
    ones = jnp.ones(cos.shape[:-1] + (rest,), F32)
    zeros = jnp.zeros(cos.shape[:-1] + (rest,), F32)
    zh = jnp.zeros_like(sin)
    ct = jnp.concatenate([cos, cos, ones], axis=-1)
    s1 = jnp.concatenate([-sin, zh, zeros], axis=-1)
    s2 = jnp.concatenate([zh, sin, zeros], axis=-1)
    return ct, s1, s2


def _row(vec, offset):
    return jnp.zeros((1, SMALL_W), F32).at[0, offset:offset + vec.shape[0]].set(vec.astype(F32))


def _layer(x2, b, s, ct, s1, s2, w_a, w_g, attn_norm, nsa_q_norm, nsa_k_norm, nsa_cmp_pos, ck_w1, ck_w2, cv_w1,
           cv_w2, lam_re, lam_im, log_dt, b_re, b_im, c_re, c_im, s5_d, s5_glu_w, conv_w, a_log, dt_bias,
           o_norm, w_branch, w_out, ffn_norm, w_gate, w_up, w_down, tiles):
    hd = HEAD_DIM
    nsa_q, nsa_kv, s5_u, gdn_qkv, gdn_z, sb_qkv, small = _inproj(x2, attn_norm.reshape(1, -1), w_a, b, s,
                                                                 tiles['tm'])
    r3 = lambda a: a.reshape(b, s, a.shape[-1])

    kv3 = r3(nsa_kv)
    grp = NSA_CMP_STRIDE * hd
    gk = kv3[:, :, 0:hd].reshape(b, s // NSA_CMP_STRIDE, grp)
    gv = kv3[:, :, hd:2 * hd].reshape(b, s // NSA_CMP_STRIDE, grp)
    pos4 = nsa_cmp_pos.reshape(4, grp)
    kc, vc, ks, vs, kw, vw = _nsa_prep(kv3, gk, gv, pos4, ck_w1.astype(BF16), ck_w2.astype(BF16),
                                       cv_w1.astype(BF16), cv_w2.astype(BF16), nsa_k_norm, ct, s1, s2)
    o_nsa = _nsa_attn(r3(nsa_q), r3(small), ct, s1, s2, nsa_q_norm.reshape(1, hd), kc, vc, ks, vs, kw, vw,
                      tiles['nsa_tq'], tiles['nsa_tk'])

    wb, are, aim, cm = _s5_params(lam_re, lam_im, log_dt, b_re, b_im, c_re, c_im)
    o_s5 = _s5(s5_u.reshape(s * b, S5_W), b, wb, are, aim, cm, s5_d.reshape(1, S5_W), s5_glu_w.astype(BF16),
               tiles['s5_steps']).reshape(b * s, S5_W)

    o_gdn = _gdn(r3(gdn_qkv), r3(small), r3(gdn_z), conv_w, _row(-jnp.exp(a_log), SMALL_GDN_A),
                 _row(dt_bias, SMALL_GDN_A), o_norm.reshape(1, hd), tiles['gdn_rows'])

    o_sb = _sb(r3(sb_qkv), tiles['sb_tq']).reshape(b * s, SB_W)

    x2 = _merge(x2, attn_norm.reshape(1, -1), w_g, (o_nsa.reshape(b * s, NSA_W), o_s5, o_gdn.reshape(b * s, GDN_W),
                                                    o_sb), w_branch.astype(BF16), w_out.astype(BF16), tiles['tm_merge'])
    return _ffn(x2, ffn_norm.reshape(1, -1), w_gate.astype(BF16), w_up.astype(BF16), w_down.astype(BF16),
                tiles['tm_ffn'], tiles['tf'])


def _tiles(b, s):
    t = b * s
    return dict(tm=min(512, t), tm_merge=min(256, t), tm_ffn=min(512, t), tf=D_FF // 2,
                nsa_tq=min(256, s), nsa_tk=min(1024, s), s5_steps=min(64, s), gdn_rows=min(256, s), sb_tq=min(256, s))


def kernel(x, positions, attn_norm, w_in, nsa_q_norm, nsa_k_norm, nsa_cmp_pos, nsa_cmp_k_w1, nsa_cmp_k_w2, nsa_cmp_v_w1, nsa_cmp_v_w2, s5_lam_re, s5_lam_im, s5_log_dt, s5_b_re, s5_b_im, s5_c_re, s5_c_im, s5_d, s5_glu_w, gdn_conv_w, gdn_a_log, gdn_dt_bias, gdn_o_norm, w_branch, w_out, ffn_norm, w_gate, w_up, w_down):
    b, s, d = x.shape
    ct, s1, s2 = _rope_tables(positions)
    tiles = _tiles(b, s)
    x2 = x.reshape(b * s, d)
    w_a, w_g = _arrange_w_in(w_in)
    per_layer = (w_a, w_g, attn_norm, nsa_q_norm, nsa_k_norm, nsa_cmp_pos, nsa_cmp_k_w1, nsa_cmp_k_w2, nsa_cmp_v_w1,
                 nsa_cmp_v_w2, s5_lam_re, s5_lam_im, s5_log_dt, s5_b_re, s5_b_im, s5_c_re, s5_c_im, s5_d, s5_glu_w,
                 gdn_conv_w, gdn_a_log, gdn_dt_bias, gdn_o_norm, w_branch, w_out, ffn_norm, w_gate, w_up, w_down)
    for l in range(attn_norm.shape[0]):
        x2 = _layer(x2, b, s, ct, s1, s2, *[p[l] for p in per_layer], tiles)
    return x2.reshape(b, s, d)
```

```python
import functools
import math

import numpy as np
import jax
import jax.numpy as jnp
from jax import lax
from jax.experimental import pallas as pl
from jax.experimental.pallas import tpu as pltpu

F32 = jnp.float32
BF16 = jnp.bfloat16

D_MODEL = 1024
HEAD_DIM = 64
ROT_DIM = HEAD_DIM // 4
ROPE_THETA = 500000.0
NORM_EPS = 1e-6
NEG_INF = -1e30
FORCED = 1e9

NSA_HEADS = 4
NSA_CMP_LEN = 32
NSA_CMP_STRIDE = 16
NSA_SLC_LEN = 64
NSA_TOPK = 16
NSA_WINDOW = 512
NSA_W = NSA_HEADS * HEAD_DIM

S5_GROUPS = 16
S5_GROUP_CH = 16
S5_W = S5_GROUPS * S5_GROUP_CH
S5_STATE = 64
S5_LANES = S5_GROUPS * S5_STATE

GDN_HEADS = 4
GDN_CONV = 4
GDN_CHUNK = 64
GDN_W = GDN_HEADS * HEAD_DIM

SB_HEADS = 4
SB_W = SB_HEADS * HEAD_DIM

N_BRANCH = 4
D_FF = 256 * math.ceil(8 * D_MODEL / (3 * 256))

LANES = 128
SUBLANES = 8

SMALL_NSA_G = 0
SMALL_GDN_A = 12
SMALL_GDN_B = 16
SMALL_W = LANES

IN_SPLITS = (NSA_W, 6 * HEAD_DIM, 3 * NSA_HEADS, S5_W, 3 * GDN_W, GDN_HEADS, GDN_HEADS, GDN_W, 3 * SB_W,
             N_BRANCH * D_MODEL)
IN_OFFSETS = tuple(int(v) for v in np.cumsum((0,) + IN_SPLITS))

LOG2E = 1.4426950408889634
LN2 = 0.6931471805599453

VMEM_LIMIT = 56 * 1024 * 1024


def _cparams(sem):
    return pltpu.CompilerParams(dimension_semantics=sem, vmem_limit_bytes=VMEM_LIMIT)


def _rms(x, gain):
    return x * lax.rsqrt(jnp.mean(x * x, axis=-1, keepdims=True) + NORM_EPS) * gain


def _gelu_tanh(x):
    return 0.5 * x * (1.0 + jnp.tanh(math.sqrt(2.0 / math.pi) * (x + 0.044715 * (x * x * x))))


def _sigmoid(x):
    return 1.0 / (1.0 + jnp.exp(-x))


def _softplus(x):
    return jnp.maximum(x, 0.0) + jnp.log(1.0 + jnp.exp(-jnp.abs(x)))


def _dot(a, b):
    return jnp.dot(a, b, preferred_element_type=F32)


def _dot_nt(a, b):
    return lax.dot_general(a, b, (((1,), (1,)), ((), ())), preferred_element_type=F32)


def _split2(x):
    hi = x.astype(BF16)
    lo = (x - hi.astype(F32)).astype(BF16)
    return hi, lo


def _split3(x):
    hi = x.astype(BF16)
    r = x - hi.astype(F32)
    mid = r.astype(BF16)
    lo = (r - mid.astype(F32)).astype(BF16)
    return hi, mid, lo


def _rope64(xn, ct, s1, s2):
    half = ROT_DIM // 2
    left = jnp.concatenate([xn[:, half:], xn[:, :half]], axis=1)
    right = jnp.concatenate([xn[:, HEAD_DIM - half:], xn[:, :HEAD_DIM - half]], axis=1)
    return xn * ct + left * s1 + right * s2


INPROJ_WIDTHS = (NSA_W, 6 * HEAD_DIM, S5_W, 3 * GDN_W, GDN_W, 3 * SB_W, SMALL_W)
INPROJ_DTYPES = (F32, F32, F32, F32, F32, BF16, F32)


def _inproj_kernel(x_ref, g_ref, w_ref, *out_refs):
    h = _rms(x_ref[...], g_ref[...]).astype(BF16)
    off = 0
    for o_ref, width in zip(out_refs, INPROJ_WIDTHS):
        o_ref[...] = _dot(h, w_ref[:, off:off + width]).astype(o_ref.dtype)
        off += width


INPROJ_S5 = 2


def _inproj(x2, gain, w_a, b, s, tm):
    t, d = x2.shape
    n = w_a.shape[1]
    nsb = s // tm
    out_specs = [pl.BlockSpec((tm, w), lambda i: (i, 0)) for w in INPROJ_WIDTHS]
    out_shape = [jax.ShapeDtypeStruct((t, w), dt) for w, dt in zip(INPROJ_WIDTHS, INPROJ_DTYPES)]
    out_specs[INPROJ_S5] = pl.BlockSpec((tm, S5_W), lambda i: (i % nsb, i // nsb))
    out_shape[INPROJ_S5] = jax.ShapeDtypeStruct((s, b * S5_W), F32)
    return pl.pallas_call(
        _inproj_kernel,
        grid=(t // tm,),
        in_specs=[pl.BlockSpec((tm, d), lambda i: (i, 0)),
                  pl.BlockSpec((1, d), lambda i: (0, 0)),
                  pl.BlockSpec((d, n), lambda i: (0, 0))],
        out_specs=out_specs,
        out_shape=out_shape,
        compiler_params=_cparams(("parallel",)),
        name="inproj",
    )(x2, gain, w_a)


_W_GROUPS = ((IN_OFFSETS[0], NSA_W), (IN_OFFSETS[1], 6 * HEAD_DIM), (IN_OFFSETS[3], S5_W), (IN_OFFSETS[4], 3 * GDN_W),
             (IN_OFFSETS[7], GDN_W), (IN_OFFSETS[8], 3 * SB_W))
_W_A = sum(w for _, w in _W_GROUPS) + SMALL_W


def _wprep_kernel(w_ref, small_ref, wa_ref, wg_ref):
    dst = 0
    for src, width in _W_GROUPS:
        wa_ref[0, :, dst:dst + width] = w_ref[0, :, src:src + width].astype(BF16)
        dst += width
    wa_ref[0, :, dst:dst + SMALL_W] = small_ref[0].astype(BF16)
    wg_ref[0] = w_ref[0, :, IN_OFFSETS[9]:IN_OFFSETS[10]].astype(BF16)


def _arrange_w_in(w_in):
    nl, d, n = w_in.shape
    o = IN_OFFSETS
    small = jnp.concatenate([w_in[:, :, o[2]:o[3]], w_in[:, :, o[5]:o[6]], w_in[:, :, o[6]:o[7]]], axis=2)
    small = jnp.pad(small, ((0, 0), (0, 0), (0, SMALL_W - small.shape[2])))
    rb = 256
    return pl.pallas_call(
        _wprep_kernel,
        grid=(nl, d // rb),
        in_specs=[pl.BlockSpec((1, rb, n), lambda l, i: (l, i, 0)),
                  pl.BlockSpec((1, rb, SMALL_W), lambda l, i: (l, i, 0))],
        out_specs=[pl.BlockSpec((1, rb, _W_A), lambda l, i: (l, i, 0)),
                   pl.BlockSpec((1, rb, N_BRANCH * d), lambda l, i: (l, i, 0))],
        out_shape=[jax.ShapeDtypeStruct((nl, d, _W_A), BF16), jax.ShapeDtypeStruct((nl, d, N_BRANCH * d), BF16)],
        compiler_params=_cparams(("parallel", "parallel")),
        name="wprep",
    )(w_in, small)


def _merge_kernel(x_ref, g_ref, wg_ref, o1_ref, o2_ref, o3_ref, o4_ref, p_ref, wo_ref, out_ref):
    x = x_ref[...]
    d = x.shape[1]
    h = _rms(x, g_ref[...]).astype(BF16)
    merged = jnp.zeros(x.shape, F32)
    for m, o_ref in enumerate((o1_ref, o2_ref, o3_ref, o4_ref)):
        gate = _sigmoid(_dot(h, wg_ref[:, m * d:(m + 1) * d]))
        merged = merged + gate * _dot(o_ref[...].astype(BF16), p_ref[m])
    out_ref[...] = x + _dot(merged.astype(BF16), wo_ref[...])


def _merge(x2, gain, w_g, outs, w_branch, w_out, tm):
    t, d = x2.shape
    bw = outs[0].shape[1]
    return pl.pallas_call(
        _merge_kernel,
        grid=(t // tm,),
        in_specs=[pl.BlockSpec((tm, d), lambda i: (i, 0)),
                  pl.BlockSpec((1, d), lambda i: (0, 0)),
                  pl.BlockSpec((d, N_BRANCH * d), lambda i: (0, 0))]
                 + [pl.BlockSpec((tm, bw), lambda i: (i, 0)) for _ in range(N_BRANCH)]
                 + [pl.BlockSpec((N_BRANCH, bw, d), lambda i: (0, 0, 0)),
                    pl.BlockSpec((d, d), lambda i: (0, 0))],
        out_specs=pl.BlockSpec((tm, d), lambda i: (i, 0)),
        out_shape=jax.ShapeDtypeStruct((t, d), F32),
        compiler_params=_cparams(("parallel",)),
        name="merge",
    )(x2, gain, w_g, *outs, w_branch, w_out)


def _ffn_kernel(x_ref, g_ref, wg_ref, wu_ref, wd_ref, out_ref, h_scr, acc_scr):
    j = pl.program_id(1)

    @pl.when(j == 0)
    def _():
        h_scr[...] = _rms(x_ref[...], g_ref[...]).astype(BF16)
        acc_scr[...] = jnp.zeros(acc_scr.shape, F32)

    h = h_scr[...]
    a = _dot(h, wg_ref[...])
    u = _dot(h, wu_ref[...])
    act = (a * _sigmoid(a) * u).astype(BF16)
    acc_scr[...] += _dot(act, wd_ref[...])

    @pl.when(j == pl.num_programs(1) - 1)
    def _():
        out_ref[...] = x_ref[...] + acc_scr[...]


def _ffn(x2, gain, w_gate, w_up, w_down, tm, tf):
    t, d = x2.shape
    f = w_gate.shape[1]
    return pl.pallas_call(
        _ffn_kernel,
        grid=(t // tm, f // tf),
        in_specs=[pl.BlockSpec((tm, d), lambda i, j: (i, 0)),
                  pl.BlockSpec((1, d), lambda i, j: (0, 0)),
                  pl.BlockSpec((d, tf), lambda i, j: (0, j)),
                  pl.BlockSpec((d, tf), lambda i, j: (0, j)),
                  pl.BlockSpec((tf, d), lambda i, j: (j, 0))],
        out_specs=pl.BlockSpec((tm, d), lambda i, j: (i, 0)),
        out_shape=jax.ShapeDtypeStruct((t, d), F32),
        scratch_shapes=[pltpu.VMEM((tm, d), BF16), pltpu.VMEM((tm, d), F32)],
        compiler_params=_cparams(("parallel", "arbitrary")),
        name="ffn",
    )(x2, gain, w_gate, w_up, w_down)


def _s5_kernel(u_ref, wb_ref, are_ref, aim_ref, cm_ref, d_ref, glu_ref, out_ref, st_ref, bu_ref, y_scr):
    nb = st_ref.shape[0]
    p = are_ref.shape[1]
    steps = u_ref.shape[0] // nb

    @pl.when(pl.program_id(0) == 0)
    def _():
        st_ref[...] = jnp.zeros(st_ref.shape, F32)

    u = u_ref[...]
    bu_ref[...] = _dot(u.astype(BF16), wb_ref[...])
    are = jnp.broadcast_to(are_ref[...], (nb, p))
    aim = jnp.broadcast_to(aim_ref[...], (nb, p))

    def body(t, carry):
        xr, xi = carry
        r0 = pl.multiple_of(t * nb, nb)
        br = bu_ref[pl.ds(r0, nb), 0:p]
        bi = bu_ref[pl.ds(r0, nb), p:2 * p]
        nr = are * xr - aim * xi + br
        ni = are * xi + aim * xr + bi
        bu_ref[pl.ds(r0, nb), 0:p] = nr
        bu_ref[pl.ds(r0, nb), p:2 * p] = ni
        return nr, ni

    xr, xi = lax.fori_loop(0, steps, body, (st_ref[:, 0:p], st_ref[:, p:2 * p]))
    st_ref[:, 0:p] = xr
    st_ref[:, p:2 * p] = xi

    y = _dot(bu_ref[...].astype(BF16), cm_ref[...]) + d_ref[...] * u
    y = _gelu_tanh(y)
    z = _dot(y.astype(BF16), glu_ref[...])
    w = z.shape[1] // 2
    y = z[:, :w] * _sigmoid(z[:, w:])
    for j in range(w // LANES):
        y_scr[j] = y[:, j * LANES:(j + 1) * LANES]
    for bi in range(nb):
        for j in range(w // LANES):
            out_ref[bi, :, j * LANES:(j + 1) * LANES] = y_scr[j, pl.ds(bi, steps, stride=nb), :]


def _s5_params(lam_re, lam_im, log_dt, b_re, b_im, c_re, c_im):
    dt = jnp.exp(log_dt)[:, None]
    mag = jnp.exp(lam_re * dt)
    ab_re, ab_im = mag * jnp.cos(lam_im * dt), mag * jnp.sin(lam_im * dt)
    den = lam_re * lam_re + lam_im * lam_im
    f_re = ((ab_re - 1.0) * lam_re + ab_im * lam_im) / den
    f_im = (ab_im * lam_re - (ab_re - 1.0) * lam_im) / den
    bb_re = f_re[..., None] * b_re - f_im[..., None] * b_im
    bb_im = f_re[..., None] * b_im + f_im[..., None] * b_re
    eye = jnp.eye(S5_GROUPS, dtype=F32)
    wb_re = jnp.einsum('gpc,gh->gchp', bb_re, eye).reshape(S5_W, S5_LANES)
    wb_im = jnp.einsum('gpc,gh->gchp', bb_im, eye).reshape(S5_W, S5_LANES)
    wb = jnp.concatenate([wb_re, wb_im], axis=1).astype(BF16)
    cm_re = jnp.einsum('gcp,gh->hpgc', c_re, eye).reshape(S5_LANES, S5_W)
    cm_im = jnp.einsum('gcp,gh->hpgc', c_im, eye).reshape(S5_LANES, S5_W)
    cm = jnp.concatenate([cm_re, -cm_im], axis=0).astype(BF16)
    return wb, ab_re.reshape(1, S5_LANES), ab_im.reshape(1, S5_LANES), cm


def _s5(u_tm, nb, wb, are, aim, cm, d_skip, glu_w, steps):
    rows, w = u_tm.shape
    r = steps * nb
    p = are.shape[1]
    s = rows // nb
    return pl.pallas_call(
        _s5_kernel,
        grid=(rows // r,),
        in_specs=[pl.BlockSpec((r, w), lambda i: (i, 0)),
                  pl.BlockSpec((w, 2 * p), lambda i: (0, 0)),
                  pl.BlockSpec((1, p), lambda i: (0, 0)),
                  pl.BlockSpec((1, p), lambda i: (0, 0)),
                  pl.BlockSpec((2 * p, w), lambda i: (0, 0)),
                  pl.BlockSpec((1, w), lambda i: (0, 0)),
                  pl.BlockSpec((w, 2 * w), lambda i: (0, 0))],
        out_specs=pl.BlockSpec((nb, steps, w), lambda i: (0, i, 0)),
        out_shape=jax.ShapeDtypeStruct((nb, s, w), F32),
        scratch_shapes=[pltpu.VMEM((nb, 2 * p), F32), pltpu.VMEM((r, 2 * p), F32),
                        pltpu.VMEM((w // LANES, r, LANES), F32)],
        compiler_params=_cparams(("arbitrary",)),
        name="s5",
    )(u_tm, wb, are, aim, cm, d_skip, glu_w)


def _sb_kernel(q_ref, k_ref, v_ref, u_ref, out_ref, q_scr, k_scr, v_scr, acc_scr, carry_scr):
    nh, hd = SB_HEADS, HEAD_DIM
    c = pl.program_id(1)
    tq = q_ref.shape[1]

    @pl.when(c == 0)
    def _():
        for h in range(nh):
            k_scr[h] = k_ref[0, :, h * hd:(h + 1) * hd]
            v_scr[h] = v_ref[0, :, h * hd:(h + 1) * hd]

    q = q_ref[0]
    for h in range(nh):
        q_scr[h] = q[:, h * hd:(h + 1) * hd] * jnp.asarray(hd ** -0.5, BF16)
    acc_scr[...] = jnp.zeros(acc_scr.shape, F32)
    carry_scr[...] = jnp.zeros(carry_scr.shape, F32)

    def chunk(k0, diagonal):
        if diagonal:
            strict = (lax.broadcasted_iota(jnp.int32, (tq, tq), 1) < lax.broadcasted_iota(jnp.int32, (tq, tq), 0))
        zs, sps = [], []
        for h in range(nh):
            z = _dot_nt(q_scr[h], k_scr[h, pl.ds(k0, tq), :])
            sp = jnp.maximum(z, 0.0) + LN2 * jnp.log2(1.0 + jnp.exp2(jnp.abs(z) * (-LOG2E)))
            if diagonal:
                sp = jnp.where(strict, sp, 0.0)
            zs.append(z)
            sps.append(sp)
        befores = [_dot(sp.astype(BF16), u_ref[...]) for sp in sps]
        for h in range(nh):
            w = jnp.exp((zs[h] - sps[h]) - befores[h] + carry_scr[h])
            if diagonal:
                w = jnp.where(strict, w, 0.0)
            acc_scr[h] += _dot(w.astype(BF16), v_scr[h, pl.ds(k0, tq), :])
            carry_scr[h] -= befores[h][:, 0:1] + sps[h][:, 0:1]

    chunk(pl.multiple_of(c * tq, tq), True)

    def body(i, _):
        chunk(pl.multiple_of((c - i) * tq, tq), False)
        return 0

    lax.fori_loop(1, c + 1, body, 0)
    out_ref[0] = jnp.concatenate([acc_scr[h] for h in range(nh)], axis=1)


def _sb_consts(tq):
    i = np.arange(tq)
    return jnp.asarray((i[:, None] > i[None, :]).astype(np.float32), BF16)


def _sb(qkv, tq):
    b, s, w3 = qkv.shape
    w = w3 // 3
    nh, hd = SB_HEADS, HEAD_DIM
    return pl.pallas_call(
        _sb_kernel,
        grid=(b, s // tq),
        in_specs=[pl.BlockSpec((1, tq, w), lambda bi, ci: (bi, ci, 0)),
                  pl.BlockSpec((1, s, w), lambda bi, ci: (bi, 0, 1)),
                  pl.BlockSpec((1, s, w), lambda bi, ci: (bi, 0, 2)),
                  pl.BlockSpec((tq, tq), lambda bi, ci: (0, 0))],
        out_specs=pl.BlockSpec((1, tq, w), lambda bi, ci: (bi, ci, 0)),
        out_shape=jax.ShapeDtypeStruct((b, s, w), F32),
        scratch_shapes=[pltpu.VMEM((nh, tq, hd), BF16), pltpu.VMEM((nh, s, hd), BF16),
                        pltpu.VMEM((nh, s, hd), BF16), pltpu.VMEM((nh, tq, hd), F32),
                        pltpu.VMEM((nh, tq, 1), F32)],
        compiler_params=_cparams(("parallel", "arbitrary")),
        name="sb",
    )(qkv, qkv, qkv, _sb_consts(tq))


def _nsa_prep_kernel(kv_ref, gk_ref, gv_ref, pos_ref, w1k_ref, w2k_ref, w1v_ref, w2v_ref, kn_ref,
                     ct_ref, s1_ref, s2_ref, kc_ref, vc_ref, ks_ref, vs_ref, kw_ref, vw_ref):
    hd = HEAD_DIM
    ncp = gk_ref.shape[1]
    half = gk_ref.shape[2]
    rowi = lax.broadcasted_iota(jnp.int32, (ncp, w1k_ref.shape[1]), 0)

    def compress(g_ref, pos_t, pos_b, w1_ref, w2_ref):
        g = g_ref[0]
        top = _dot((g + pos_t).astype(BF16), w1_ref[0:half, :])
        bot = _dot((g + pos_b).astype(BF16), w1_ref[half:2 * half, :])
        bot = jnp.where(rowi < ncp - 1, pltpu.roll(bot, ncp - 1, 0), 0.0)
        return _dot(_gelu_tanh(top + bot).astype(BF16), w2_ref[...])

    kc = compress(gk_ref, pos_ref[0:1, :], pos_ref[1:2, :], w1k_ref, w2k_ref)
    kc_ref[0] = _rms(kc, kn_ref[0:1, :]).astype(BF16)
    vc_ref[0] = compress(gv_ref, pos_ref[2:3, :], pos_ref[3:4, :], w1v_ref, w2v_ref).astype(BF16)

    ct, s1, s2 = ct_ref[0], s1_ref[0], s2_ref[0]
    ones_col = (lax.broadcasted_iota(jnp.int32, (kv_ref.shape[1], hd), 1) == 0).astype(BF16)
    k_s = kv_ref[0, :, 2 * hd:3 * hd]
    ks_ref[0] = _rope64(_rms(k_s, kn_ref[1:2, :]), ct, s1, s2).astype(BF16)
    vs_ref[0] = jnp.concatenate([kv_ref[0, :, 3 * hd:4 * hd].astype(BF16), ones_col], axis=1)
    k_w = kv_ref[0, :, 4 * hd:5 * hd]
    kw_ref[0] = _rope64(_rms(k_w, kn_ref[2:3, :]), ct, s1, s2).astype(BF16)
    vw_ref[0] = jnp.concatenate([kv_ref[0, :, 5 * hd:6 * hd].astype(BF16), ones_col], axis=1)


def _nsa_prep(kv, gk, gv, pos4, w1k, w2k, w1v, w2v, k_norm, ct, s1, s2):
    b, s, _ = kv.shape
    ncp, half = gk.shape[1], gk.shape[2]
    hid = w1k.shape[1]
    hd = HEAD_DIM
    full3 = lambda shp: pl.BlockSpec((1,) + shp, lambda i: (i, 0, 0))
    const2 = lambda shp: pl.BlockSpec(shp, lambda i: (0, 0))
    return pl.pallas_call(
        _nsa_prep_kernel,
        grid=(b,),
        in_specs=[full3((s, 6 * hd)), full3((ncp, half)), full3((ncp, half)), const2((4, half)),
                  const2((2 * half, hid)), const2((hid, hd)), const2((2 * half, hid)), const2((hid, hd)),
                  const2((3, hd)), full3((s, hd)), full3((s, hd)), full3((s, hd))],
        out_specs=[full3((ncp, hd)), full3((ncp, hd)), full3((s, hd)), full3((s, 2 * hd)), full3((s, hd)),
                   full3((s, 2 * hd))],
        out_shape=[jax.ShapeDtypeStruct((b, ncp, hd), BF16), jax.ShapeDtypeStruct((b, ncp, hd), BF16)]
                  + [jax.ShapeDtypeStruct((b, s, hd), BF16), jax.ShapeDtypeStruct((b, s, 2 * hd), BF16)] * 2,
        compiler_params=_cparams(("parallel",)),
        name="nsa_prep",
    )(kv, gk, gv, pos4, w1k, w2k, w1v, w2v, k_norm, ct, s1, s2)


def _nsa_attn_kernel(q_ref, small_ref, ct_ref, s1_ref, s2_ref, qn_ref, kc_ref, vc_ref, ks_ref, vs_ref,
                     kw_ref, vw_ref, ot_ref, e3_ref, out_ref, imp_scr, qr_scr, m_scr, acc_scr):
    nh, hd = NSA_HEADS, HEAD_DIM
    tq = q_ref.shape[1]
    rows = nh * tq
    ncp = kc_ref.shape[1]
    ns = ot_ref.shape[0]
    tk = e3_ref.shape[2]
    top_n = min(NSA_TOPK, ns)
    scale = HEAD_DIM ** -0.5
    c = pl.program_id(1)
    t0 = c * tq

    q = q_ref[0]
    ct, s1, s2 = ct_ref[0], s1_ref[0], s2_ref[0]
    qn_l, qr_l = [], []
    for h in range(nh):
        qn = _rms(q[:, h * hd:(h + 1) * hd], qn_ref[...])
        qn_l.append((qn * scale).astype(BF16))
        qr_l.append((_rope64(qn, ct, s1, s2) * scale).astype(BF16))
    qn_all = jnp.concatenate(qn_l, axis=0)

    def tvec(shape):
        return t0 + (lax.broadcasted_iota(jnp.int32, shape, 0) & (tq - 1))

    s = _dot_nt(qn_all, kc_ref[0])
    cmp_end = lax.broadcasted_iota(jnp.int32, (rows, ncp), 1) * NSA_CMP_STRIDE + (NSA_CMP_LEN - 1)
    mask = cmp_end <= tvec((rows, ncp))
    s = jnp.where(mask, s, NEG_INF)
    p = jnp.where(mask, jnp.exp(s - jnp.max(s, axis=-1, keepdims=True)), 0.0)
    p = p / jnp.maximum(jnp.sum(p, axis=-1, keepdims=True), 1e-30)
    o_c = _dot(p.astype(BF16), vc_ref[0])

    psum = p[0:tq]
    for h in range(1, nh):
        psum = psum + p[h * tq:(h + 1) * tq]
    ph, plo = _split2(psum)
    imp = _dot_nt(ot_ref[...], ph) + _dot_nt(ot_ref[...], plo)
    jb = lax.broadcasted_iota(jnp.int32, (ns, tq), 0)
    tl = t0 + lax.broadcasted_iota(jnp.int32, (ns, tq), 1)
    forced = (jb == (tl // NSA_SLC_LEN)) | (jb == 0)
    imp = jnp.where(forced, FORCED, jnp.where(jb * NSA_SLC_LEN <= tl, imp, NEG_INF))
    imp_scr[...] = imp
    cnt = jnp.zeros((ns, tq), F32)
    for i in range(ns):
        bi = imp_scr[i:i + 1, :]
        ge = jnp.where(bi >= imp, 1.0, 0.0)
        gt = jnp.where(bi > imp, 1.0, 0.0)
        cnt = cnt + jnp.where(jb > i, ge, gt)
    sel = jnp.where(cnt < top_n, 1.0, 0.0).T.astype(BF16)

    def finish(acc):
        return acc[:, :hd] / jnp.maximum(acc[:, hd:hd + 1], 1e-30)

    m_scr[...] = jnp.full(m_scr.shape, NEG_INF, F32)
    acc_scr[...] = jnp.zeros(acc_scr.shape, F32)
    for h in range(nh):
        qr_scr[h] = qr_l[h]

    def sel_tile(kt, causal):
        k0 = pl.multiple_of(kt * tk, tk)
        bias = (_dot(sel, e3_ref[kt]) - 1.0) * (-NEG_INF)
        if causal:
            kpos = k0 + lax.broadcasted_iota(jnp.int32, (tq, tk), 1)
            bias = jnp.where(kpos <= t0 + lax.broadcasted_iota(jnp.int32, (tq, tk), 0), bias, NEG_INF)
        k = ks_ref[0, pl.ds(k0, tk), :]
        v = vs_ref[0, pl.ds(k0, tk), :]
        scs, m_olds, m_news, pts = {}, {}, {}, {}
        for step in range(nh + 2):
            if step < nh:
                scs[step] = _dot_nt(qr_scr[step], k) + bias
            h = step - 1
            if 0 <= h < nh:
                m_olds[h] = m_scr[h]
                m_news[h] = jnp.maximum(m_olds[h], jnp.max(scs[h], axis=-1, keepdims=True))
                pts[h] = jnp.exp(scs[h] - m_news[h]).astype(BF16)
            h = step - 2
            if 0 <= h < nh:
                acc_scr[h] = jnp.exp(m_olds[h] - m_news[h]) * acc_scr[h] + _dot(pts[h], v)
                m_scr[h] = m_news[h]

    n_full = t0 // tk

    def sel_body(kt, _):
        sel_tile(kt, False)
        return 0

    lax.fori_loop(0, n_full, sel_body, 0)
    sel_tile(n_full, True)
    o_s = [finish(acc_scr[h]) for h in range(nh)]

    n_w = NSA_WINDOW // tq + 1
    tvw = t0 + lax.broadcasted_iota(jnp.int32, (tq, tq), 0)
    w_bias, w_k0 = [], []
    for i in range(n_w):
        kstart = t0 - NSA_WINDOW + i * tq
        kpos = kstart + lax.broadcasted_iota(jnp.int32, (tq, tq), 1)
        inside = (kpos >= jnp.maximum(tvw - (NSA_WINDOW - 1), 0)) & (kpos <= tvw)
        w_bias.append(jnp.where(inside, 0.0, NEG_INF))
        w_k0.append(pl.multiple_of(jnp.maximum(kstart, 0), tq))
    w_sc = [[_dot_nt(qr_scr[h], kw_ref[0, pl.ds(w_k0[i], tq), :]) + w_bias[i] for i in range(n_w)]
            for h in range(nh)]
    w_m = []
    for h in range(nh):
        tile_max = w_sc[h][0]
        for i in range(1, n_w):
            tile_max = jnp.maximum(tile_max, w_sc[h][i])
        w_m.append(jnp.max(tile_max, axis=-1, keepdims=True))
    o_w = []
    for h in range(nh):
        acc = _dot(jnp.exp(w_sc[h][0] - w_m[h]).astype(BF16), vw_ref[0, pl.ds(w_k0[0], tq), :])
        for i in range(1, n_w):
            acc = acc + _dot(jnp.exp(w_sc[h][i] - w_m[h]).astype(BF16), vw_ref[0, pl.ds(w_k0[i], tq), :])
        o_w.append(finish(acc))

    g = _sigmoid(small_ref[0])
    outs = []
    for h in range(nh):
        b0 = SMALL_NSA_G + 3 * h
        outs.append(g[:, b0:b0 + 1] * o_c[h * tq:(h + 1) * tq] + g[:, b0 + 1:b0 + 2] * o_s[h]
                    + g[:, b0 + 2:b0 + 3] * o_w[h])
    out_ref[0] = jnp.concatenate(outs, axis=1)


def _nsa_consts(s, tk):
    ncp = s // NSA_CMP_STRIDE
    ns = s // NSA_SLC_LEN
    n_cmp = (s - NSA_CMP_LEN) // NSA_CMP_STRIDE + 1
    cmp_start = np.arange(ncp) * NSA_CMP_STRIDE
    slc_start = np.arange(ns) * NSA_SLC_LEN
    overlap = ((cmp_start[:, None] < slc_start[None, :] + NSA_SLC_LEN)
               & (cmp_start[:, None] + NSA_CMP_LEN > slc_start[None, :])).astype(np.float32)
    overlap[n_cmp:] = 0.0
    kidx = np.arange(s)
    e = (kidx[None, :] // NSA_SLC_LEN == np.arange(ns)[:, None]).astype(np.float32)
    e3 = e.reshape(ns, s // tk, tk).transpose(1, 0, 2)
    return jnp.asarray(overlap.T, BF16), jnp.asarray(e3, BF16)


def _nsa_attn(q, small, ct, s1, s2, q_norm, kc, vc, ks, vs, kw, vw, tq, tk):
    b, s, w = q.shape
    hd = HEAD_DIM
    ncp = kc.shape[1]
    ns = s // NSA_SLC_LEN
    ot, e3 = _nsa_consts(s, tk)
    nh = NSA_HEADS
    blk = lambda width: pl.BlockSpec((1, tq, width), lambda bi, ci: (bi, ci, 0))
    per_b = lambda n, width=hd: pl.BlockSpec((1, n, width), lambda bi, ci: (bi, 0, 0))
    return pl.pallas_call(
        _nsa_attn_kernel,
        grid=(b, s // tq),
        in_specs=[blk(w), blk(SMALL_W), blk(hd), blk(hd), blk(hd),
                  pl.BlockSpec((1, hd), lambda bi, ci: (0, 0)),
                  per_b(ncp), per_b(ncp), per_b(s), per_b(s, 2 * hd), per_b(s), per_b(s, 2 * hd),
                  pl.BlockSpec((ns, ncp), lambda bi, ci: (0, 0)),
                  pl.BlockSpec((s // tk, ns, tk), lambda bi, ci: (0, 0, 0))],
        out_specs=blk(w),
        out_shape=jax.ShapeDtypeStruct((b, s, w), F32),
        scratch_shapes=[pltpu.VMEM((ns, tq), F32), pltpu.VMEM((nh, tq, hd), BF16), pltpu.VMEM((nh, tq, 1), F32),
                        pltpu.VMEM((nh, tq, 2 * hd), F32)],
        compiler_params=_cparams(("parallel", "arbitrary")),
        name="nsa_attn",
    )(q, small, ct, s1, s2, q_norm, kc, vc, ks, vs, kw, vw, ot, e3)


def _gdn_kernel(x_ref, prev_ref, small_ref, z_ref, cw_ref, nega_ref, dtb_ref, on_ref, tril_ref, out_ref, st_ref):
    nh, hd, ck = GDN_HEADS, HEAD_DIM, GDN_CHUNK
    r = x_ref.shape[1]
    width = x_ref.shape[2]
    i_blk = pl.program_id(1)

    @pl.when(i_blk == 0)
    def _():
        st_ref[...] = jnp.zeros(st_ref.shape, F32)

    x = x_ref[0]
    prev = jnp.where(i_blk > 0, prev_ref[0], 0.0)
    r8 = lax.broadcasted_iota(jnp.int32, (SUBLANES, width), 0)
    conv = x * cw_ref[GDN_CONV - 1:GDN_CONV, :]
    for back in range(1, GDN_CONV):
        xs = pltpu.roll(x, back, 0)
        head = jnp.where(r8 < back, pltpu.roll(prev, back, 0), xs[0:SUBLANES])
        xs = jnp.concatenate([head, xs[SUBLANES:]], axis=0)
        conv = conv + xs * cw_ref[GDN_CONV - 1 - back:GDN_CONV - back, :]
    act = conv * _sigmoid(conv)

    sm = small_ref[0]
    beta_all = _sigmoid(sm)
    g_all = nega_ref[...] * _softplus(sm + dtb_ref[...])
    zf = z_ref[0]
    tril = tril_ref[...]
    ri = lax.broadcasted_iota(jnp.int32, (ck, ck), 0)
    ci = lax.broadcasted_iota(jnp.int32, (ck, ck), 1)
    lower = ri >= ci
    strict = ri > ci

    eye = (ri == ci).astype(F32)
    right = lax.broadcasted_iota(jnp.int32, (ck, 2 * ck), 1) >= ck
    nchunk = r // ck
    inst = [(cidx, h) for cidx in range(nchunk) for h in range(nh)]

    gcums = []
    for cidx in range(nchunk):
        g3 = _split3(g_all[cidx * ck:(cidx + 1) * ck])
        gcum = _dot(tril, g3[0]) + (_dot(tril, g3[1]) + _dot(tril, g3[2]))
        gcums.append((gcum, gcum.T))
    pre = []
    for cidx, h in inst:
        rs = slice(cidx * ck, (cidx + 1) * ck)
        gcum, gcum_t = gcums[cidx]
        q = act[rs, h * hd:(h + 1) * hd]
        k = act[rs, GDN_W + h * hd:GDN_W + (h + 1) * hd]
        v = act[rs, 2 * GDN_W + h * hd:2 * GDN_W + (h + 1) * hd]
        q = q * lax.rsqrt(jnp.sum(q * q, axis=-1, keepdims=True) + NORM_EPS) * (HEAD_DIM ** -0.5)
        k = k * lax.rsqrt(jnp.sum(k * k, axis=-1, keepdims=True) + NORM_EPS)
        beta = beta_all[rs, SMALL_GDN_B + h:SMALL_GDN_B + h + 1]
        gc = gcum[:, SMALL_GDN_A + h:SMALL_GDN_A + h + 1]
        gr = gcum_t[SMALL_GDN_A + h:SMALL_GDN_A + h + 1, :]
        g_last = gcum[ck - 1:ck, SMALL_GDN_A + h:SMALL_GDN_A + h + 1]
        decay = jnp.where(lower, jnp.exp(jnp.where(lower, gc - gr, 0.0)), 0.0)
        kb = k * beta
        e_gc = jnp.exp(gc)
        rhs = jnp.concatenate([v * beta, kb * e_gc], axis=1).astype(BF16)
        kq = jnp.concatenate([kb, q], axis=0).astype(BF16)
        pre.append(dict(kq=kq, k16=k.astype(BF16), decay=decay, rhs=rhs, q_dec=q * e_gc,
                        k_dec_t=(k * jnp.exp(g_last - gc)).T.astype(BF16), e_last=jnp.exp(g_last)))
    for p in pre:
        raw = _dot_nt(p['kq'], p['k16'])
        l_mat = jnp.where(strict, raw[:ck] * p['decay'], 0.0)
        p['a_intra'] = jnp.where(lower, raw[ck:] * p['decay'], 0.0).astype(BF16)
        p['mp'] = jnp.concatenate([l_mat, eye], axis=1)
    for rnd in range(int(math.ceil(math.log2(ck)))):
        for p in pre:
            mp = p['mp']
            res = _dot(mp[:, :ck].astype(BF16), mp.astype(BF16))
            if rnd == 0:
                res = jnp.where(right, -res, res)
            p['mp'] = res + jnp.where(right, mp, 0.0)
    for p in pre:
        uw = _dot(p['mp'][:, ck:].astype(BF16), p['rhs'])
        ak = jnp.concatenate([p['a_intra'], p['k_dec_t']], axis=0)
        res = _dot(ak, uw.astype(BF16))
        p['o_const'] = res[:ck, :hd]
        p['s_const'] = res[ck:, :hd]
        p['mq'] = jnp.concatenate([-res[ck:, hd:], p['q_dec'] - res[:ck, hd:]], axis=0).astype(BF16)

    out_chunks = []
    for cidx in range(nchunk):
        rs = slice(cidx * ck, (cidx + 1) * ck)
        heads_out = []
        for h in range(nh):
            p = pre[cidx * nh + h]
            state = st_ref[h]
            r2 = _dot(p['mq'], state.astype(BF16))
            o = r2[hd:] + p['o_const']
            st_ref[h] = state * p['e_last'] + (r2[:hd] + p['s_const'])
            zh = zf[rs, h * hd:(h + 1) * hd]
            heads_out.append(_rms(o, on_ref[...]) * (zh * _sigmoid(zh)))
        out_chunks.append(jnp.concatenate(heads_out, axis=1))
    out_ref[0] = jnp.concatenate(out_chunks, axis=0)


def _gdn(qkv, small, z, conv_w, neg_a_row, dtb_row, o_norm, r):
    b, s, width = qkv.shape
    ck = GDN_CHUNK
    tril = jnp.asarray(np.tril(np.ones((ck, ck), np.float32)), BF16)
    nprev = r // SUBLANES
    return pl.pallas_call(
        _gdn_kernel,
        grid=(b, s // r),
        in_specs=[pl.BlockSpec((1, r, width), lambda bi, i: (bi, i, 0)),
                  pl.BlockSpec((1, SUBLANES, width), lambda bi, i: (bi, jnp.maximum(i * nprev - 1, 0), 0)),
                  pl.BlockSpec((1, r, SMALL_W), lambda bi, i: (bi, i, 0)),
                  pl.BlockSpec((1, r, GDN_W), lambda bi, i: (bi, i, 0)),
                  pl.BlockSpec((GDN_CONV, width), lambda bi, i: (0, 0)),
                  pl.BlockSpec((1, SMALL_W), lambda bi, i: (0, 0)),
                  pl.BlockSpec((1, SMALL_W), lambda bi, i: (0, 0)),
                  pl.BlockSpec((1, HEAD_DIM), lambda bi, i: (0, 0)),
                  pl.BlockSpec((ck, ck), lambda bi, i: (0, 0))],
        out_specs=pl.BlockSpec((1, r, GDN_W), lambda bi, i: (bi, i, 0)),
        out_shape=jax.ShapeDtypeStruct((b, s, GDN_W), F32),
        scratch_shapes=[pltpu.VMEM((GDN_HEADS, HEAD_DIM, HEAD_DIM), F32)],
        compiler_params=_cparams(("parallel", "arbitrary")),
        name="gdn",
    )(qkv, qkv, small, z, conv_w, neg_a_row, dtb_row, o_norm, tril)


def _rope_tables(positions):
    half = ROT_DIM // 2
    inv_freq = ROPE_THETA ** (-jnp.arange(half, dtype=F32) / half)
    ang = positions.astype(F32)[..., None] * inv_freq
    cos, sin = jnp.cos(ang), jnp.sin(ang)
    rest = HEAD_DIM - ROT_DIM
    ones = jnp.ones(cos.shape[:-1] + (rest,), F32)
    zeros = jnp.zeros(cos.shape[:-1] + (rest,), F32)
    zh = jnp.zeros_like(sin)
    ct = jnp.concatenate([cos, cos, ones], axis=-1)
    s1 = jnp.concatenate([-sin, zh, zeros], axis=-1)
    s2 = jnp.concatenate([zh, sin, zeros], axis=-1)
    return ct, s1, s2


def _row(vec, offset):
    return jnp.zeros((1, SMALL_W), F32).at[0, offset:offset + vec.shape[0]].set(vec.astype(F32))


def _layer(x2, b, s, ct, s1, s2, w_a, w_g, attn_norm, nsa_q_norm, nsa_k_norm, nsa_cmp_pos, ck_w1, ck_w2, cv_w1,
           cv_w2, lam_re, lam_im, log_dt, b_re, b_im, c_re, c_im, s5_d, s5_glu_w, conv_w, a_log, dt_bias,
           o_norm, w_branch, w_out, ffn_norm, w_gate, w_up, w_down, tiles):
    hd = HEAD_DIM
    nsa_q, nsa_kv, s5_u, gdn_qkv, gdn_z, sb_qkv, small = _inproj(x2, attn_norm.reshape(1, -1), w_a, b, s,
                                                                 tiles['tm'])
    r3 = lambda a: a.reshape(b, s, a.shape[-1])

    kv3 = r3(nsa_kv)
    grp = NSA_CMP_STRIDE * hd
    gk = kv3[:, :, 0:hd].reshape(b, s // NSA_CMP_STRIDE, grp)
    gv = kv3[:, :, hd:2 * hd].reshape(b, s // NSA_CMP_STRIDE, grp)
    pos4 = nsa_cmp_pos.reshape(4, grp)
    kc, vc, ks, vs, kw, vw = _nsa_prep(kv3, gk, gv, pos4, ck_w1.astype(BF16), ck_w2.astype(BF16),
                                       cv_w1.astype(BF16), cv_w2.astype(BF16), nsa_k_norm, ct, s1, s2)
    o_nsa = _nsa_attn(r3(nsa_q), r3(small), ct, s1, s2, nsa_q_norm.reshape(1, hd), kc, vc, ks, vs, kw, vw,
                      tiles['nsa_tq'], tiles['nsa_tk'])

    wb, are, aim, cm = _s5_params(lam_re, lam_im, log_dt, b_re, b_im, c_re, c_im)
    o_s5 = _s5(s5_u.reshape(s * b, S5_W), b, wb, are, aim, cm, s5_d.reshape(1, S5_W), s5_glu_w.astype(BF16),
               tiles['s5_steps']).reshape(b * s, S5_W)

    o_gdn = _gdn(r3(gdn_qkv), r3(small), r3(gdn_z), conv_w, _row(-jnp.exp(a_log), SMALL_GDN_A),
                 _row(dt_bias, SMALL_GDN_A), o_norm.reshape(1, hd), tiles['gdn_rows'])

    o_sb = _sb(r3(sb_qkv), tiles['sb_tq']).reshape(b * s, SB_W)

    x2 = _merge(x2, attn_norm.reshape(1, -1), w_g, (o_nsa.reshape(b * s, NSA_W), o_s5, o_gdn.reshape(b * s, GDN_W),
                                                    o_sb), w_branch.astype(BF16), w_out.astype(BF16), tiles['tm_merge'])
    return _ffn(x2, ffn_norm.reshape(1, -1), w_gate.astype(BF16), w_up.astype(BF16), w_down.astype(BF16),
                tiles['tm_ffn'], tiles['tf'])


def _tiles(b, s):
    t = b * s
    return dict(tm=min(512, t), tm_merge=min(256, t), tm_ffn=min(512, t), tf=D_FF // 2,
                nsa_tq=min(256, s), nsa_tk=min(1024, s), s5_steps=min(64, s), gdn_rows=min(256, s), sb_tq=min(256, s))


def kernel(x, positions, attn_norm, w_in, nsa_q_norm, nsa_k_norm, nsa_cmp_pos, nsa_cmp_k_w1, nsa_cmp_k_w2, nsa_cmp_v_w1, nsa_cmp_v_w2, s5_lam_re, s5_lam_im, s5_log_dt, s5_b_re, s5_b_im, s5_c_re, s5_c_im, s5_d, s5_glu_w, gdn_conv_w, gdn_a_log, gdn_dt_bias, gdn_o_norm, w_branch, w_out, ffn_norm, w_gate, w_up, w_down):
    b, s, d = x.shape
    ct, s1, s2 = _rope_tables(positions)
    tiles = _tiles(b, s)
    x2 = x.reshape(b * s, d)
    w_a, w_g = _arrange_w_in(w_in)
    per_layer = (w_a, w_g, attn_norm, nsa_q_norm, nsa_k_norm, nsa_cmp_pos, nsa_cmp_k_w1, nsa_cmp_k_w2, nsa_cmp_v_w1,
                 nsa_cmp_v_w2, s5_lam_re, s5_lam_im, s5_log_dt, s5_b_re, s5_b_im, s5_c_re, s5_c_im, s5_d, s5_glu_w,
                 gdn_conv_w, gdn_a_log, gdn_dt_bias, gdn_o_norm, w_branch, w_out, ffn_norm, w_gate, w_up, w_down)
    for l in range(attn_norm.shape[0]):
        x2 = _layer(x2, b, s, ct, s1, s2, *[p[l] for p in per_layer], tiles)
    return x2.reshape(b, s, d)
```

```python
import math

import numpy as np
import jax
import jax.numpy as jnp
from jax import lax
from jax.experimental import pallas as pl
from jax.experimental.pallas import tpu as pltpu

F32 = jnp.float32
BF16 = jnp.bfloat16

D_MODEL = 1024
HEAD_DIM = 64
ROT_DIM = HEAD_DIM // 4
ROPE_THETA = 500000.0
NORM_EPS = 1e-6
NEG_INF = -1e30
FORCED = 1e9

NSA_HEADS = 4
NSA_CMP_LEN = 32
NSA_CMP_STRIDE = 16
NSA_SLC_LEN = 64
NSA_TOPK = 16
NSA_WINDOW = 512
NSA_W = NSA_HEADS * HEAD_DIM

S5_GROUPS = 16
S5_GROUP_CH = 16
S5_W = S5_GROUPS * S5_GROUP_CH
S5_STATE = 64
S5_LANES = S5_GROUPS * S5_STATE

GDN_HEADS = 4
GDN_CONV = 4
GDN_CHUNK = 64
GDN_W = GDN_HEADS * HEAD_DIM

SB_HEADS = 4
SB_W = SB_HEADS * HEAD_DIM

N_BRANCH = 4
D_FF = 256 * math.ceil(8 * D_MODEL / (3 * 256))

LANES = 128
SUBLANES = 8

SMALL_NSA_G = 0
SMALL_GDN_A = 12
SMALL_GDN_B = 16
SMALL_W = LANES

IN_SPLITS = (NSA_W, 6 * HEAD_DIM, 3 * NSA_HEADS, S5_W, 3 * GDN_W, GDN_HEADS, GDN_HEADS, GDN_W, 3 * SB_W,
             N_BRANCH * D_MODEL)
IN_OFFSETS = tuple(int(v) for v in np.cumsum((0,) + IN_SPLITS))

LOG2E = 1.4426950408889634

VMEM_LIMIT = 56 * 1024 * 1024


def _cparams(sem):
    return pltpu.CompilerParams(dimension_semantics=sem, vmem_limit_bytes=VMEM_LIMIT)


def _rms(x, gain):
    return x * lax.rsqrt(jnp.mean(x * x, axis=-1, keepdims=True) + NORM_EPS) * gain


def _gelu_tanh(x):
    return 0.5 * x * (1.0 + jnp.tanh(math.sqrt(2.0 / math.pi) * (x + 0.044715 * (x * x * x))))


def _sigmoid(x):
    return 1.0 / (1.0 + jnp.exp(-x))


def _softplus(x):
    return jnp.maximum(x, 0.0) + jnp.log(1.0 + jnp.exp(-jnp.abs(x)))


def _dot(a, b):
    return jnp.dot(a, b, preferred_element_type=F32)


def _dot_nt(a, b):
    return lax.dot_general(a, b, (((1,), (1,)), ((), ())), preferred_element_type=F32)


def _split2(x):
    hi = x.astype(BF16)
    lo = (x - hi.astype(F32)).astype(BF16)
    return hi, lo


def _split3(x):
    hi = x.astype(BF16)
    r = x - hi.astype(F32)
    mid = r.astype(BF16)
    lo = (r - mid.astype(F32)).astype(BF16)
    return hi, mid, lo


def _rope64(xn, ct, s1, s2):
    half = ROT_DIM // 2
    left = jnp.concatenate([xn[:, half:], xn[:, :half]], axis=1)
    right = jnp.concatenate([xn[:, HEAD_DIM - half:], xn[:, :HEAD_DIM - half]], axis=1)
    return xn * ct + left * s1 + right * s2


INPROJ_WIDTHS = (NSA_W, 6 * HEAD_DIM, S5_W, 3 * GDN_W, GDN_W, 3 * SB_W, SMALL_W)
INPROJ_DTYPES = (F32, F32, F32, F32, F32, BF16, F32)


def _inproj_kernel(x_ref, g_ref, w_ref, *out_refs):
    h = _rms(x_ref[...], g_ref[...]).astype(BF16)
    off = 0
    for o_ref, width in zip(out_refs, INPROJ_WIDTHS):
        o_ref[...] = _dot(h, w_ref[:, off:off + width]).astype(o_ref.dtype)
        off += width


INPROJ_S5 = 2


def _inproj(x2, gain, w_a, b, s, tm):
    t, d = x2.shape
    n = w_a.shape[1]
    nsb = s // tm
    out_specs = [pl.BlockSpec((tm, w), lambda i: (i, 0)) for w in INPROJ_WIDTHS]
    out_shape = [jax.ShapeDtypeStruct((t, w), dt) for w, dt in zip(INPROJ_WIDTHS, INPROJ_DTYPES)]
    out_specs[INPROJ_S5] = pl.BlockSpec((tm, S5_W), lambda i: (i % nsb, i // nsb))
    out_shape[INPROJ_S5] = jax.ShapeDtypeStruct((s, b * S5_W), F32)
    return pl.pallas_call(
        _inproj_kernel,
        grid=(t // tm,),
        in_specs=[pl.BlockSpec((tm, d), lambda i: (i, 0)),
                  pl.BlockSpec((1, d), lambda i: (0, 0)),
                  pl.BlockSpec((d, n), lambda i: (0, 0))],
        out_specs=out_specs,
        out_shape=out_shape,
        compiler_params=_cparams(("parallel",)),
        name="inproj",
    )(x2, gain, w_a)


_W_GROUPS = ((IN_OFFSETS[0], NSA_W), (IN_OFFSETS[1], 6 * HEAD_DIM), (IN_OFFSETS[3], S5_W), (IN_OFFSETS[4], 3 * GDN_W),
             (IN_OFFSETS[7], GDN_W), (IN_OFFSETS[8], 3 * SB_W))
_W_A = sum(w for _, w in _W_GROUPS) + SMALL_W


def _wprep_kernel(w_ref, small_ref, wa_ref, wg_ref):
    dst = 0
    for src, width in _W_GROUPS:
        wa_ref[0, :, dst:dst + width] = w_ref[0, :, src:src + width].astype(BF16)
        dst += width
    wa_ref[0, :, dst:dst + SMALL_W] = small_ref[0].astype(BF16)
    wg_ref[0] = w_ref[0, :, IN_OFFSETS[9]:IN_OFFSETS[10]].astype(BF16)


def _arrange_w_in(w_in):
    nl, d, n = w_in.shape
    o = IN_OFFSETS
    small = jnp.concatenate([w_in[:, :, o[2]:o[3]], w_in[:, :, o[5]:o[6]], w_in[:, :, o[6]:o[7]]], axis=2)
    small = jnp.pad(small, ((0, 0), (0, 0), (0, SMALL_W - small.shape[2])))
    rb = 256
    return pl.pallas_call(
        _wprep_kernel,
        grid=(nl, d // rb),
        in_specs=[pl.BlockSpec((1, rb, n), lambda l, i: (l, i, 0)),
                  pl.BlockSpec((1, rb, SMALL_W), lambda l, i: (l, i, 0))],
        out_specs=[pl.BlockSpec((1, rb, _W_A), lambda l, i: (l, i, 0)),
                   pl.BlockSpec((1, rb, N_BRANCH * d), lambda l, i: (l, i, 0))],
        out_shape=[jax.ShapeDtypeStruct((nl, d, _W_A), BF16), jax.ShapeDtypeStruct((nl, d, N_BRANCH * d), BF16)],
        compiler_params=_cparams(("parallel", "parallel")),
        name="wprep",
    )(w_in, small)


def _merge_kernel(x_ref, g_ref, wg_ref, o1_ref, o2_ref, o3_ref, o4_ref, p_ref, wo_ref, out_ref):
    x = x_ref[...]
    d = x.shape[1]
    h = _rms(x, g_ref[...]).astype(BF16)
    merged = jnp.zeros(x.shape, F32)
    for m, o_ref in enumerate((o1_ref, o2_ref, o3_ref, o4_ref)):
        gate = _sigmoid(_dot(h, wg_ref[:, m * d:(m + 1) * d]))
        merged = merged + gate * _dot(o_ref[...].astype(BF16), p_ref[m])
    out_ref[...] = x + _dot(merged.astype(BF16), wo_ref[...])


def _merge(x2, gain, w_g, outs, w_branch, w_out, tm):
    t, d = x2.shape
    bw = outs[0].shape[1]
    return pl.pallas_call(
        _merge_kernel,
        grid=(t // tm,),
        in_specs=[pl.BlockSpec((tm, d), lambda i: (i, 0)),
                  pl.BlockSpec((1, d), lambda i: (0, 0)),
                  pl.BlockSpec((d, N_BRANCH * d), lambda i: (0, 0))]
                 + [pl.BlockSpec((tm, bw), lambda i: (i, 0)) for _ in range(N_BRANCH)]
                 + [pl.BlockSpec((N_BRANCH, bw, d), lambda i: (0, 0, 0)),
                    pl.BlockSpec((d, d), lambda i: (0, 0))],
        out_specs=pl.BlockSpec((tm, d), lambda i: (i, 0)),
        out_shape=jax.ShapeDtypeStruct((t, d), F32),
        compiler_params=_cparams(("parallel",)),
        name="merge",
    )(x2, gain, w_g, *outs, w_branch, w_out)


def _ffn_kernel(x_ref, g_ref, wg_ref, wu_ref, wd_ref, out_ref, h_scr, acc_scr):
    j = pl.program_id(1)

    @pl.when(j == 0)
    def _():
        h_scr[...] = _rms(x_ref[...], g_ref[...]).astype(BF16)
        acc_scr[...] = jnp.zeros(acc_scr.shape, F32)

    h = h_scr[...]
    a = _dot(h, wg_ref[...])
    u = _dot(h, wu_ref[...])
    act = (a * _sigmoid(a) * u).astype(BF16)
    acc_scr[...] += _dot(act, wd_ref[...])

    @pl.when(j == pl.num_programs(1) - 1)
    def _():
        out_ref[...] = x_ref[...] + acc_scr[...]


def _ffn(x2, gain, w_gate, w_up, w_down, tm, tf):
    t, d = x2.shape
    f = w_gate.shape[1]
    return pl.pallas_call(
        _ffn_kernel,
        grid=(t // tm, f // tf),
        in_specs=[pl.BlockSpec((tm, d), lambda i, j: (i, 0)),
                  pl.BlockSpec((1, d), lambda i, j: (0, 0)),
                  pl.BlockSpec((d, tf), lambda i, j: (0, j)),
                  pl.BlockSpec((d, tf), lambda i, j: (0, j)),
                  pl.BlockSpec((tf, d), lambda i, j: (j, 0))],
        out_specs=pl.BlockSpec((tm, d), lambda i, j: (i, 0)),
        out_shape=jax.ShapeDtypeStruct((t, d), F32),
        scratch_shapes=[pltpu.VMEM((tm, d), BF16), pltpu.VMEM((tm, d), F32)],
        compiler_params=_cparams(("parallel", "arbitrary")),
        name="ffn",
    )(x2, gain, w_gate, w_up, w_down)


def _s5_kernel(u_ref, wb_ref, are_ref, aim_ref, cm_ref, d_ref, glu_ref, out_ref, st_ref, bu_ref, y_scr):
    nb = st_ref.shape[0]
    p = are_ref.shape[1]
    steps = u_ref.shape[0] // nb

    @pl.when(pl.program_id(0) == 0)
    def _():
        st_ref[...] = jnp.zeros(st_ref.shape, F32)

    u = u_ref[...]
    bu_ref[...] = _dot(u.astype(BF16), wb_ref[...])
    are = jnp.broadcast_to(are_ref[...], (nb, p))
    aim = jnp.broadcast_to(aim_ref[...], (nb, p))

    def body(t, carry):
        xr, xi = carry
        r0 = pl.multiple_of(t * nb, nb)
        br = bu_ref[pl.ds(r0, nb), 0:p]
        bi = bu_ref[pl.ds(r0, nb), p:2 * p]
        nr = are * xr - aim * xi + br
        ni = are * xi + aim * xr + bi
        bu_ref[pl.ds(r0, nb), 0:p] = nr
        bu_ref[pl.ds(r0, nb), p:2 * p] = ni
        return nr, ni

    xr, xi = lax.fori_loop(0, steps, body, (st_ref[:, 0:p], st_ref[:, p:2 * p]))
    st_ref[:, 0:p] = xr
    st_ref[:, p:2 * p] = xi

    y = _dot(bu_ref[...].astype(BF16), cm_ref[...]) + d_ref[...] * u
    y = _gelu_tanh(y)
    z = _dot(y.astype(BF16), glu_ref[...])
    w = z.shape[1] // 2
    y = z[:, :w] * _sigmoid(z[:, w:])
    for j in range(w // LANES):
        y_scr[j] = y[:, j * LANES:(j + 1) * LANES]
    for bi in range(nb):
        for j in range(w // LANES):
            out_ref[bi, :, j * LANES:(j + 1) * LANES] = y_scr[j, pl.ds(bi, steps, stride=nb), :]


def _s5_params(lam_re, lam_im, log_dt, b_re, b_im, c_re, c_im):
    dt = jnp.exp(log_dt)[:, None]
    mag = jnp.exp(lam_re * dt)
    ab_re, ab_im = mag * jnp.cos(lam_im * dt), mag * jnp.sin(lam_im * dt)
    den = lam_re * lam_re + lam_im * lam_im
    f_re = ((ab_re - 1.0) * lam_re + ab_im * lam_im) / den
    f_im = (ab_im * lam_re - (ab_re - 1.0) * lam_im) / den
    bb_re = f_re[..., None] * b_re - f_im[..., None] * b_im
    bb_im = f_re[..., None] * b_im + f_im[..., None] * b_re
    eye = jnp.eye(S5_GROUPS, dtype=F32)
    wb_re = jnp.einsum('gpc,gh->gchp', bb_re, eye).reshape(S5_W, S5_LANES)
    wb_im = jnp.einsum('gpc,gh->gchp', bb_im, eye).reshape(S5_W, S5_LANES)
    wb = jnp.concatenate([wb_re, wb_im], axis=1).astype(BF16)
    cm_re = jnp.einsum('gcp,gh->hpgc', c_re, eye).reshape(S5_LANES, S5_W)
    cm_im = jnp.einsum('gcp,gh->hpgc', c_im, eye).reshape(S5_LANES, S5_W)
    cm = jnp.concatenate([cm_re, -cm_im], axis=0).astype(BF16)
    return wb, ab_re.reshape(1, S5_LANES), ab_im.reshape(1, S5_LANES), cm


def _s5(u_tm, nb, wb, are, aim, cm, d_skip, glu_w, steps):
    rows, w = u_tm.shape
    r = steps * nb
    p = are.shape[1]
    s = rows // nb
    return pl.pallas_call(
        _s5_kernel,
        grid=(rows // r,),
        in_specs=[pl.BlockSpec((r, w), lambda i: (i, 0)),
                  pl.BlockSpec((w, 2 * p), lambda i: (0, 0)),
                  pl.BlockSpec((1, p), lambda i: (0, 0)),
                  pl.BlockSpec((1, p), lambda i: (0, 0)),
                  pl.BlockSpec((2 * p, w), lambda i: (0, 0)),
                  pl.BlockSpec((1, w), lambda i: (0, 0)),
                  pl.BlockSpec((w, 2 * w), lambda i: (0, 0))],
        out_specs=pl.BlockSpec((nb, steps, w), lambda i: (0, i, 0)),
        out_shape=jax.ShapeDtypeStruct((nb, s, w), F32),
        scratch_shapes=[pltpu.VMEM((nb, 2 * p), F32), pltpu.VMEM((r, 2 * p), F32),
                        pltpu.VMEM((w // LANES, r, LANES), F32)],
        compiler_params=_cparams(("arbitrary",)),
        name="s5",
    )(u_tm, wb, are, aim, cm, d_skip, glu_w)


def _sb_kernel(q_ref, k_ref, v_ref, u_ref, out_ref, q_scr, k_scr, v_scr, acc_scr, carry_scr):
    nh, hd = SB_HEADS, HEAD_DIM
    c = pl.program_id(1)
    tq = q_ref.shape[1]

    @pl.when(c == 0)
    def _():
        for h in range(nh):
            k_scr[h] = k_ref[0, :, h * hd:(h + 1) * hd]
            v_scr[h] = v_ref[0, :, h * hd:(h + 1) * hd]

    q = q_ref[0]
    for h in range(nh):
        q_scr[h] = (q[:, h * hd:(h + 1) * hd].astype(F32) * (hd ** -0.5 * LOG2E)).astype(BF16)
    acc_scr[...] = jnp.zeros(acc_scr.shape, F32)
    carry_scr[...] = jnp.zeros(carry_scr.shape, F32)

    def chunk(k0, diagonal):
        if diagonal:
            strict = (lax.broadcasted_iota(jnp.int32, (tq, tq), 1) < lax.broadcasted_iota(jnp.int32, (tq, tq), 0))
        zs, sps = [], []
        for h in range(nh):
            z = _dot_nt(q_scr[h], k_scr[h, pl.ds(k0, tq), :])
            sp = jnp.maximum(z, 0.0) + jnp.log2(1.0 + jnp.exp2(-jnp.abs(z)))
            if diagonal:
                sp = jnp.where(strict, sp, 0.0)
            zs.append(z)
            sps.append(sp)
        befores = [_dot(sp.astype(BF16), u_ref[...]) for sp in sps]
        for h in range(nh):
            w = jnp.exp2((zs[h] - sps[h]) - befores[h] + carry_scr[h])
            if diagonal:
                w = jnp.where(strict, w, 0.0)
            acc_scr[h] += _dot(w.astype(BF16), v_scr[h, pl.ds(k0, tq), :])
            carry_scr[h] -= befores[h][:, 0:1] + sps[h][:, 0:1]

    chunk(pl.multiple_of(c * tq, tq), True)

    def body(i, _):
        chunk(pl.multiple_of((c - i) * tq, tq), False)
        return 0

    lax.fori_loop(1, c + 1, body, 0)
    out_ref[0] = jnp.concatenate([acc_scr[h] for h in range(nh)], axis=1)


def _sb_consts(tq):
    i = np.arange(tq)
    return jnp.asarray((i[:, None] > i[None, :]).astype(np.float32), BF16)


def _sb(qkv, tq):
    b, s, w3 = qkv.shape
    w = w3 // 3
    nh, hd = SB_HEADS, HEAD_DIM
    return pl.pallas_call(
        _sb_kernel,
        grid=(b, s // tq),
        in_specs=[pl.BlockSpec((1, tq, w), lambda bi, ci: (bi, ci, 0)),
                  pl.BlockSpec((1, s, w), lambda bi, ci: (bi, 0, 1)),
                  pl.BlockSpec((1, s, w), lambda bi, ci: (bi, 0, 2)),
                  pl.BlockSpec((tq, tq), lambda bi, ci: (0, 0))],
        out_specs=pl.BlockSpec((1, tq, w), lambda bi, ci: (bi, ci, 0)),
        out_shape=jax.ShapeDtypeStruct((b, s, w), F32),
        scratch_shapes=[pltpu.VMEM((nh, tq, hd), BF16), pltpu.VMEM((nh, s, hd), BF16),
                        pltpu.VMEM((nh, s, hd), BF16), pltpu.VMEM((nh, tq, hd), F32),
                        pltpu.VMEM((nh, tq, 1), F32)],
        compiler_params=_cparams(("parallel", "arbitrary")),
        name="sb",
    )(qkv, qkv, qkv, _sb_consts(tq))


def _nsa_prep_kernel(kv_ref, gk_ref, gv_ref, pos_ref, w1k_ref, w2k_ref, w1v_ref, w2v_ref, kn_ref,
                     ct_ref, s1_ref, s2_ref, kc_ref, vc_ref, ks_ref, vs_ref, kw_ref, vw_ref):
    hd = HEAD_DIM
    ncp = gk_ref.shape[1]
    half = gk_ref.shape[2]
    rowi = lax.broadcasted_iota(jnp.int32, (ncp, w1k_ref.shape[1]), 0)

    def compress(g_ref, pos_t, pos_b, w1_ref, w2_ref):
        g = g_ref[0]
        top = _dot((g + pos_t).astype(BF16), w1_ref[0:half, :])
        bot = _dot((g + pos_b).astype(BF16), w1_ref[half:2 * half, :])
        bot = jnp.where(rowi < ncp - 1, pltpu.roll(bot, ncp - 1, 0), 0.0)
        return _dot(_gelu_tanh(top + bot).astype(BF16), w2_ref[...])

    kc = compress(gk_ref, pos_ref[0:1, :], pos_ref[1:2, :], w1k_ref, w2k_ref)
    kc_ref[0] = _rms(kc, kn_ref[0:1, :]).astype(BF16)
    vc_ref[0] = compress(gv_ref, pos_ref[2:3, :], pos_ref[3:4, :], w1v_ref, w2v_ref).astype(BF16)

    ct, s1, s2 = ct_ref[0], s1_ref[0], s2_ref[0]
    ones_col = (lax.broadcasted_iota(jnp.int32, (kv_ref.shape[1], hd), 1) == 0).astype(BF16)
    k_s = kv_ref[0, :, 2 * hd:3 * hd]
    ks_ref[0] = _rope64(_rms(k_s, kn_ref[1:2, :]), ct, s1, s2).astype(BF16)
    vs_ref[0] = jnp.concatenate([kv_ref[0, :, 3 * hd:4 * hd].astype(BF16), ones_col], axis=1)
    k_w = kv_ref[0, :, 4 * hd:5 * hd]
    kw_ref[0] = _rope64(_rms(k_w, kn_ref[2:3, :]), ct, s1, s2).astype(BF16)
    vw_ref[0] = jnp.concatenate([kv_ref[0, :, 5 * hd:6 * hd].astype(BF16), ones_col], axis=1)


def _nsa_prep(kv, gk, gv, pos4, w1k, w2k, w1v, w2v, k_norm, ct, s1, s2):
    b, s, _ = kv.shape
    ncp, half = gk.shape[1], gk.shape[2]
    hid = w1k.shape[1]
    hd = HEAD_DIM
    full3 = lambda shp: pl.BlockSpec((1,) + shp, lambda i: (i, 0, 0))
    const2 = lambda shp: pl.BlockSpec(shp, lambda i: (0, 0))
    return pl.pallas_call(
        _nsa_prep_kernel,
        grid=(b,),
        in_specs=[full3((s, 6 * hd)), full3((ncp, half)), full3((ncp, half)), const2((4, half)),
                  const2((2 * half, hid)), const2((hid, hd)), const2((2 * half, hid)), const2((hid, hd)),
                  const2((3, hd)), full3((s, hd)), full3((s, hd)), full3((s, hd))],
        out_specs=[full3((ncp, hd)), full3((ncp, hd)), full3((s, hd)), full3((s, 2 * hd)), full3((s, hd)),
                   full3((s, 2 * hd))],
        out_shape=[jax.ShapeDtypeStruct((b, ncp, hd), BF16), jax.ShapeDtypeStruct((b, ncp, hd), BF16)]
                  + [jax.ShapeDtypeStruct((b, s, hd), BF16), jax.ShapeDtypeStruct((b, s, 2 * hd), BF16)] * 2,
        compiler_params=_cparams(("parallel",)),
        name="nsa_prep",
    )(kv, gk, gv, pos4, w1k, w2k, w1v, w2v, k_norm, ct, s1, s2)


def _nsa_attn_kernel(q_ref, small_ref, ct_ref, s1_ref, s2_ref, qn_ref, kc_ref, vc_ref, ks_ref, vs_ref,
                     kw_ref, vw_ref, ot_ref, e3_ref, out_ref, imp_scr, qr_scr, m_scr, acc_scr):
    nh, hd = NSA_HEADS, HEAD_DIM
    tq = q_ref.shape[1]
    rows = nh * tq
    ncp = kc_ref.shape[1]
    ns = ot_ref.shape[0]
    tk = e3_ref.shape[2]
    top_n = min(NSA_TOPK, ns)
    scale = HEAD_DIM ** -0.5
    c = pl.program_id(1)
    t0 = c * tq

    q = q_ref[0]
    ct, s1, s2 = ct_ref[0], s1_ref[0], s2_ref[0]
    qn_l, qr_l = [], []
    for h in range(nh):
        qn = _rms(q[:, h * hd:(h + 1) * hd], qn_ref[...])
        qn_l.append((qn * scale).astype(BF16))
        qr_l.append((_rope64(qn, ct, s1, s2) * scale).astype(BF16))
    qn_all = jnp.concatenate(qn_l, axis=0)

    def tvec(shape):
        return t0 + (lax.broadcasted_iota(jnp.int32, shape, 0) & (tq - 1))

    s = _dot_nt(qn_all, kc_ref[0])
    cmp_end = lax.broadcasted_iota(jnp.int32, (rows, ncp), 1) * NSA_CMP_STRIDE + (NSA_CMP_LEN - 1)
    mask = cmp_end <= tvec((rows, ncp))
    s = jnp.where(mask, s, NEG_INF)
    p = jnp.where(mask, jnp.exp(s - jnp.max(s, axis=-1, keepdims=True)), 0.0)
    p = p / jnp.maximum(jnp.sum(p, axis=-1, keepdims=True), 1e-30)
    o_c = _dot(p.astype(BF16), vc_ref[0])

    psum = p[0:tq]
    for h in range(1, nh):
        psum = psum + p[h * tq:(h + 1) * tq]
    ph, plo = _split2(psum)
    imp = _dot_nt(ot_ref[...], ph) + _dot_nt(ot_ref[...], plo)
    jb = lax.broadcasted_iota(jnp.int32, (ns, tq), 0)
    tl = t0 + lax.broadcasted_iota(jnp.int32, (ns, tq), 1)
    forced = (jb == (tl // NSA_SLC_LEN)) | (jb == 0)
    imp = jnp.where(forced, FORCED, jnp.where(jb * NSA_SLC_LEN <= tl, imp, NEG_INF))
    imp_scr[...] = imp
    cnt = jnp.zeros((ns, tq), F32)
    for i in range(ns):
        bi = imp_scr[i:i + 1, :]
        ge = jnp.where(bi >= imp, 1.0, 0.0)
        gt = jnp.where(bi > imp, 1.0, 0.0)
        cnt = cnt + jnp.where(jb > i, ge, gt)
    sel = jnp.where(cnt < top_n, 1.0, 0.0).T.astype(BF16)

    def finish(acc):
        return acc[:, :hd] / jnp.maximum(acc[:, hd:hd + 1], 1e-30)

    m_scr[...] = jnp.full(m_scr.shape, NEG_INF, F32)
    acc_scr[...] = jnp.zeros(acc_scr.shape, F32)
    for h in range(nh):
        qr_scr[h] = qr_l[h]

    def sel_tile(kt, causal):
        k0 = pl.multiple_of(kt * tk, tk)
        bias = (_dot(sel, e3_ref[kt]) - 1.0) * (-NEG_INF)
        if causal:
            kpos = k0 + lax.broadcasted_iota(jnp.int32, (tq, tk), 1)
            bias = jnp.where(kpos <= t0 + lax.broadcasted_iota(jnp.int32, (tq, tk), 0), bias, NEG_INF)
        k = ks_ref[0, pl.ds(k0, tk), :]
        v = vs_ref[0, pl.ds(k0, tk), :]
        scs, m_olds, m_news, pts = {}, {}, {}, {}
        for step in range(nh + 2):
            if step < nh:
                scs[step] = _dot_nt(qr_scr[step], k) + bias
            h = step - 1
            if 0 <= h < nh:
                m_olds[h] = m_scr[h]
                m_news[h] = jnp.maximum(m_olds[h], jnp.max(scs[h], axis=-1, keepdims=True))
                pts[h] = jnp.exp(scs[h] - m_news[h]).astype(BF16)
            h = step - 2
            if 0 <= h < nh:
                acc_scr[h] = jnp.exp(m_olds[h] - m_news[h]) * acc_scr[h] + _dot(pts[h], v)
                m_scr[h] = m_news[h]

    n_full = t0 // tk

    def sel_body(kt, _):
        sel_tile(kt, False)
        return 0

    lax.fori_loop(0, n_full, sel_body, 0)
    sel_tile(n_full, True)
    o_s = [finish(acc_scr[h]) for h in range(nh)]

    n_w = NSA_WINDOW // tq + 1
    tvw = t0 + lax.broadcasted_iota(jnp.int32, (tq, tq), 0)
    w_bias, w_k0 = [], []
    for i in range(n_w):
        kstart = t0 - NSA_WINDOW + i * tq
        kpos = kstart + lax.broadcasted_iota(jnp.int32, (tq, tq), 1)
        inside = (kpos >= jnp.maximum(tvw - (NSA_WINDOW - 1), 0)) & (kpos <= tvw)
        w_bias.append(jnp.where(inside, 0.0, NEG_INF))
        w_k0.append(pl.multiple_of(jnp.maximum(kstart, 0), tq))
    w_sc = [[_dot_nt(qr_scr[h], kw_ref[0, pl.ds(w_k0[i], tq), :]) + w_bias[i] for i in range(n_w)]
            for h in range(nh)]
    w_m = []
    for h in range(nh):
        tile_max = w_sc[h][0]
        for i in range(1, n_w):
            tile_max = jnp.maximum(tile_max, w_sc[h][i])
        w_m.append(jnp.max(tile_max, axis=-1, keepdims=True))
    o_w = []
    for h in range(nh):
        acc = _dot(jnp.exp(w_sc[h][0] - w_m[h]).astype(BF16), vw_ref[0, pl.ds(w_k0[0], tq), :])
        for i in range(1, n_w):
            acc = acc + _dot(jnp.exp(w_sc[h][i] - w_m[h]).astype(BF16), vw_ref[0, pl.ds(w_k0[i], tq), :])
        o_w.append(finish(acc))

    g = _sigmoid(small_ref[0])
    outs = []
    for h in range(nh):
        b0 = SMALL_NSA_G + 3 * h
        outs.append(g[:, b0:b0 + 1] * o_c[h * tq:(h + 1) * tq] + g[:, b0 + 1:b0 + 2] * o_s[h]
                    + g[:, b0 + 2:b0 + 3] * o_w[h])
    out_ref[0] = jnp.concatenate(outs, axis=1)


def _nsa_consts(s, tk):
    ncp = s // NSA_CMP_STRIDE
    ns = s // NSA_SLC_LEN
    n_cmp = (s - NSA_CMP_LEN) // NSA_CMP_STRIDE + 1
    cmp_start = np.arange(ncp) * NSA_CMP_STRIDE
    slc_start = np.arange(ns) * NSA_SLC_LEN
    overlap = ((cmp_start[:, None] < slc_start[None, :] + NSA_SLC_LEN)
               & (cmp_start[:, None] + NSA_CMP_LEN > slc_start[None, :])).astype(np.float32)
    overlap[n_cmp:] = 0.0
    kidx = np.arange(s)
    e = (kidx[None, :] // NSA_SLC_LEN == np.arange(ns)[:, None]).astype(np.float32)
    e3 = e.reshape(ns, s // tk, tk).transpose(1, 0, 2)
    return jnp.asarray(overlap.T, BF16), jnp.asarray(e3, BF16)


def _nsa_attn(q, small, ct, s1, s2, q_norm, kc, vc, ks, vs, kw, vw, tq, tk):
    b, s, w = q.shape
    hd = HEAD_DIM
    ncp = kc.shape[1]
    ns = s // NSA_SLC_LEN
    ot, e3 = _nsa_consts(s, tk)
    nh = NSA_HEADS
    blk = lambda width: pl.BlockSpec((1, tq, width), lambda bi, ci: (bi, ci, 0))
    per_b = lambda n, width=hd: pl.BlockSpec((1, n, width), lambda bi, ci: (bi, 0, 0))
    return pl.pallas_call(
        _nsa_attn_kernel,
        grid=(b, s // tq),
        in_specs=[blk(w), blk(SMALL_W), blk(hd), blk(hd), blk(hd),
                  pl.BlockSpec((1, hd), lambda bi, ci: (0, 0)),
                  per_b(ncp), per_b(ncp), per_b(s), per_b(s, 2 * hd), per_b(s), per_b(s, 2 * hd),
                  pl.BlockSpec((ns, ncp), lambda bi, ci: (0, 0)),
                  pl.BlockSpec((s // tk, ns, tk), lambda bi, ci: (0, 0, 0))],
        out_specs=blk(w),
        out_shape=jax.ShapeDtypeStruct((b, s, w), F32),
        scratch_shapes=[pltpu.VMEM((ns, tq), F32), pltpu.VMEM((nh, tq, hd), BF16), pltpu.VMEM((nh, tq, 1), F32),
                        pltpu.VMEM((nh, tq, 2 * hd), F32)],
        compiler_params=_cparams(("parallel", "arbitrary")),
        name="nsa_attn",
    )(q, small, ct, s1, s2, q_norm, kc, vc, ks, vs, kw, vw, ot, e3)


def _gdn_kernel(x_ref, prev_ref, small_ref, z_ref, cw_ref, nega_ref, dtb_ref, on_ref, tril_ref, out_ref, st_ref):
    nh, hd, ck = GDN_HEADS, HEAD_DIM, GDN_CHUNK
    r = x_ref.shape[1]
    width = x_ref.shape[2]
    i_blk = pl.program_id(1)

    @pl.when(i_blk == 0)
    def _():
        st_ref[...] = jnp.zeros(st_ref.shape, F32)

    x = x_ref[0]
    prev = jnp.where(i_blk > 0, prev_ref[0], 0.0)
    r8 = lax.broadcasted_iota(jnp.int32, (SUBLANES, width), 0)
    conv = x * cw_ref[GDN_CONV - 1:GDN_CONV, :]
    for back in range(1, GDN_CONV):
        xs = pltpu.roll(x, back, 0)
        head = jnp.where(r8 < back, pltpu.roll(prev, back, 0), xs[0:SUBLANES])
        xs = jnp.concatenate([head, xs[SUBLANES:]], axis=0)
        conv = conv + xs * cw_ref[GDN_CONV - 1 - back:GDN_CONV - back, :]
    act = conv * _sigmoid(conv)

    sm = small_ref[0]
    beta_all = _sigmoid(sm)
    g_all = nega_ref[...] * _softplus(sm + dtb_ref[...])
    zf = z_ref[0]
    tril = tril_ref[...]
    ri = lax.broadcasted_iota(jnp.int32, (ck, ck), 0)
    ci = lax.broadcasted_iota(jnp.int32, (ck, ck), 1)
    lower = ri >= ci
    strict = ri > ci

    eye = (ri == ci).astype(F32)
    right = lax.broadcasted_iota(jnp.int32, (ck, 2 * ck), 1) >= ck
    nchunk = r // ck
    inst = [(cidx, h) for cidx in range(nchunk) for h in range(nh)]

    gcums = []
    for cidx in range(nchunk):
        g3 = _split3(g_all[cidx * ck:(cidx + 1) * ck])
        gcum = _dot(tril, g3[0]) + (_dot(tril, g3[1]) + _dot(tril, g3[2]))
        gcums.append((gcum, gcum.T))
    pre = []
    for cidx, h in inst:
        rs = slice(cidx * ck, (cidx + 1) * ck)
        gcum, gcum_t = gcums[cidx]
        q = act[rs, h * hd:(h + 1) * hd]
        k = act[rs, GDN_W + h * hd:GDN_W + (h + 1) * hd]
        v = act[rs, 2 * GDN_W + h * hd:2 * GDN_W + (h + 1) * hd]
        q = q * lax.rsqrt(jnp.sum(q * q, axis=-1, keepdims=True) + NORM_EPS) * (HEAD_DIM ** -0.5)
        k = k * lax.rsqrt(jnp.sum(k * k, axis=-1, keepdims=True) + NORM_EPS)
        beta = beta_all[rs, SMALL_GDN_B + h:SMALL_GDN_B + h + 1]
        gc = gcum[:, SMALL_GDN_A + h:SMALL_GDN_A + h + 1]
        gr = gcum_t[SMALL_GDN_A + h:SMALL_GDN_A + h + 1, :]
        g_last = gcum[ck - 1:ck, SMALL_GDN_A + h:SMALL_GDN_A + h + 1]
        decay = jnp.where(lower, jnp.exp(jnp.where(lower, gc - gr, 0.0)), 0.0)
        kb = k * beta
        e_gc = jnp.exp(gc)
        rhs = jnp.concatenate([v * beta, kb * e_gc], axis=1).astype(BF16)
        kq = jnp.concatenate([kb, q], axis=0).astype(BF16)
        pre.append(dict(kq=kq, k16=k.astype(BF16), decay=decay, rhs=rhs, q_dec=q * e_gc,
                        k_dec_t=(k * jnp.exp(g_last - gc)).T.astype(BF16), e_last=jnp.exp(g_last)))
    for p in pre:
        raw = _dot_nt(p['kq'], p['k16'])
        l_mat = jnp.where(strict, raw[:ck] * p['decay'], 0.0)
        p['a_intra'] = jnp.where(lower, raw[ck:] * p['decay'], 0.0).astype(BF16)
        p['mp'] = jnp.concatenate([l_mat, eye], axis=1)
    for rnd in range(int(math.ceil(math.log2(ck)))):
        for p in pre:
            mp = p['mp']
            res = _dot(mp[:, :ck].astype(BF16), mp.astype(BF16))
            if rnd == 0:
                res = jnp.where(right, -res, res)
            p['mp'] = res + jnp.where(right, mp, 0.0)
    for p in pre:
        uw = _dot(p['mp'][:, ck:].astype(BF16), p['rhs'])
        ak = jnp.concatenate([p['a_intra'], p['k_dec_t']], axis=0)
        res = _dot(ak, uw.astype(BF16))
        p['o_const'] = res[:ck, :hd]
        p['s_const'] = res[ck:, :hd]
        p['mq'] = jnp.concatenate([-res[ck:, hd:], p['q_dec'] - res[:ck, hd:]], axis=0).astype(BF16)

    out_chunks = []
    for cidx in range(nchunk):
        rs = slice(cidx * ck, (cidx + 1) * ck)
        heads_out = []
        for h in range(nh):
            p = pre[cidx * nh + h]
            state = st_ref[h]
            r2 = _dot(p['mq'], state.astype(BF16))
            o = r2[hd:] + p['o_const']
            st_ref[h] = state * p['e_last'] + (r2[:hd] + p['s_const'])
            zh = zf[rs, h * hd:(h + 1) * hd]
            heads_out.append(_rms(o, on_ref[...]) * (zh * _sigmoid(zh)))
        out_chunks.append(jnp.concatenate(heads_out, axis=1))
    out_ref[0] = jnp.concatenate(out_chunks, axis=0)


def _gdn(qkv, small, z, conv_w, neg_a_row, dtb_row, o_norm, r):
    b, s, width = qkv.shape
    ck = GDN_CHUNK
    tril = jnp.asarray(np.tril(np.ones((ck, ck), np.float32)), BF16)
    nprev = r // SUBLANES
    return pl.pallas_call(
        _gdn_kernel,
        grid=(b, s // r),
        in_specs=[pl.BlockSpec((1, r, width), lambda bi, i: (bi, i, 0)),
                  pl.BlockSpec((1, SUBLANES, width), lambda bi, i: (bi, jnp.maximum(i * nprev - 1, 0), 0)),
                  pl.BlockSpec((1, r, SMALL_W), lambda bi, i: (bi, i, 0)),
                  pl.BlockSpec((1, r, GDN_W), lambda bi, i: (bi, i, 0)),
                  pl.BlockSpec((GDN_CONV, width), lambda bi, i: (0, 0)),
                  pl.BlockSpec((1, SMALL_W), lambda bi, i: (0, 0)),
                  pl.BlockSpec((1, SMALL_W), lambda bi, i: (0, 0)),
                  pl.BlockSpec((1, HEAD_DIM), lambda bi, i: (0, 0)),
                  pl.BlockSpec((ck, ck), lambda bi, i: (0, 0))],
        out_specs=pl.BlockSpec((1, r, GDN_W), lambda bi, i: (bi, i, 0)),
        out_shape=jax.ShapeDtypeStruct((b, s, GDN_W), F32),
        scratch_shapes=[pltpu.VMEM((GDN_HEADS, HEAD_DIM, HEAD_DIM), F32)],
        compiler_params=_cparams(("parallel", "arbitrary")),
        name="gdn",
    )(qkv, qkv, small, z, conv_w, neg_a_row, dtb_row, o_norm, tril)


def _rope_tables(positions):
    half = ROT_DIM // 2
    inv_freq = ROPE_THETA ** (-jnp.arange(half, dtype=F32) / half)
    ang = positions.astype(F32)[..., None] * inv_freq
    cos, sin = jnp.cos(ang), jnp.sin(ang)
    rest = HEAD_DIM - ROT_DIM
    ones = jnp.ones(cos.shape[:-1] + (rest,), F32)
    zeros = jnp.zeros(cos.shape[:-1] + (rest,), F32)
    zh = jnp.zeros_like(sin)
    ct = jnp.concatenate([cos, cos, ones], axis=-1)
    s1 = jnp.concatenate([-sin, zh, zeros], axis=-1)
    s2 = jnp.concatenate([zh, sin, zeros], axis=-1)
    return ct, s1, s2


def _row(vec, offset):
    return jnp.zeros((1, SMALL_W), F32).at[0, offset:offset + vec.shape[0]].set(vec.astype(F32))


def _layer(x2, b, s, ct, s1, s2, w_a, w_g, attn_norm, nsa_q_norm, nsa_k_norm, nsa_cmp_pos, ck_w1, ck_w2, cv_w1,
           cv_w2, lam_re, lam_im, log_dt, b_re, b_im, c_re, c_im, s5_d, s5_glu_w, conv_w, a_log, dt_bias,
           o_norm, w_branch, w_out, ffn_norm, w_gate, w_up, w_down, tiles):
    hd = HEAD_DIM
    nsa_q, nsa_kv, s5_u, gdn_qkv, gdn_z, sb_qkv, small = _inproj(x2, attn_norm.reshape(1, -1), w_a, b, s,
                                                                 tiles['tm'])
    r3 = lambda a: a.reshape(b, s, a.shape[-1])

    kv3 = r3(nsa_kv)
    grp = NSA_CMP_STRIDE * hd
    gk = kv3[:, :, 0:hd].reshape(b, s // NSA_CMP_STRIDE, grp)
    gv = kv3[:, :, hd:2 * hd].reshape(b, s // NSA_CMP_STRIDE, grp)
    pos4 = nsa_cmp_pos.reshape(4, grp)
    kc, vc, ks, vs, kw, vw = _nsa_prep(kv3, gk, gv, pos4, ck_w1.astype(BF16), ck_w2.astype(BF16),
                                       cv_w1.astype(BF16), cv_w2.astype(BF16), nsa_k_norm, ct, s1, s2)
    o_nsa = _nsa_attn(r3(nsa_q), r3(small), ct, s1, s2, nsa_q_norm.reshape(1, hd), kc, vc, ks, vs, kw, vw,
                      tiles['nsa_tq'], tiles['nsa_tk'])

    wb, are, aim, cm = _s5_params(lam_re, lam_im, log_dt, b_re, b_im, c_re, c_im)
    o_s5 = _s5(s5_u.reshape(s * b, S5_W), b, wb, are, aim, cm, s5_d.reshape(1, S5_W), s5_glu_w.astype(BF16),
               tiles['s5_steps']).reshape(b * s, S5_W)

    o_gdn = _gdn(r3(gdn_qkv), r3(small), r3(gdn_z), conv_w, _row(-jnp.exp(a_log), SMALL_GDN_A),
                 _row(dt_bias, SMALL_GDN_A), o_norm.reshape(1, hd), tiles['gdn_rows'])

    o_sb = _sb(r3(sb_qkv), tiles['sb_tq']).reshape(b * s, SB_W)

    x2 = _merge(x2, attn_norm.reshape(1, -1), w_g, (o_nsa.reshape(b * s, NSA_W), o_s5, o_gdn.reshape(b * s, GDN_W),
                                                    o_sb), w_branch.astype(BF16), w_out.astype(BF16), tiles['tm_merge'])
    return _ffn(x2, ffn_norm.reshape(1, -1), w_gate.astype(BF16), w_up.astype(BF16), w_down.astype(BF16),
                tiles['tm_ffn'], tiles['tf'])


def _tiles(b, s):
    t = b * s
    return dict(tm=min(512, t), tm_merge=min(256, t), tm_ffn=min(512, t), tf=D_FF // 2,
                nsa_tq=min(256, s), nsa_tk=min(1024, s), s5_steps=min(64, s), gdn_rows=min(256, s), sb_tq=min(256, s))


def kernel(x, positions, attn_norm, w_in, nsa_q_norm, nsa_k_norm, nsa_cmp_pos, nsa_cmp_k_w1, nsa_cmp_k_w2, nsa_cmp_v_w1, nsa_cmp_v_w2, s5_lam_re, s5_lam_im, s5_log_dt, s5_b_re, s5_b_im, s5_c_re, s5_c_im, s5_d, s5_glu_w, gdn_conv_w, gdn_a_log, gdn_dt_bias, gdn_o_norm, w_branch, w_out, ffn_norm, w_gate, w_up, w_down):
    b, s, d = x.shape
    ct, s1, s2 = _rope_tables(positions)
    tiles = _tiles(b, s)
    x2 = x.reshape(b * s, d)
    w_a, w_g = _arrange_w_in(w_in)
    per_layer = (w_a, w_g, attn_norm, nsa_q_norm, nsa_k_norm, nsa_cmp_pos, nsa_cmp_k_w1, nsa_cmp_k_w2, nsa_cmp_v_w1,
                 nsa_cmp_v_w2, s5_lam_re, s5_lam_im, s5_log_dt, s5_b_re, s5_b_im, s5_c_re, s5_c_im, s5_d, s5_glu_w,
                 gdn_conv_w, gdn_a_log, gdn_dt_bias, gdn_o_norm, w_branch, w_out, ffn_norm, w_gate, w_up, w_down)
    for l in range(attn_norm.shape[0]):
        x2 = _layer(x2, b, s, ct, s1, s2, *[p[l] for p in per_layer], tiles)
    return x2.reshape(b, s, d)
```
